```python
import jax, jax.numpy as jnp
from jax import lax
import numpy as np

D_MODEL = 2048
BATCH = 8
SEQ = 4096
DEPTH = 2
DEC_BATCH = 8
DEC_SEQ = 16
PAST_LEN = 2048

CHUNK = 64
Q_BLOCK = 128
H_MLA = 8
NOPE_DIM = 128
ROPE_DIM = 64
V_DIM = 128
Q_LORA = 512
KV_LORA = 256
ROPE_THETA = 10000.0
H_BAND = 8
D_BAND = 128
LEFT_CHUNKS = 8
BAND_LEN = (LEFT_CHUNKS + 1) * CHUNK
BAND_PAST = LEFT_CHUNKS * CHUNK
REL_CLIP = 128
MLA_WIDTH = H_MLA * V_DIM
BAND_WIDTH = H_BAND * D_BAND
MIX_WIDTH = MLA_WIDTH + BAND_WIDTH
IN_WIDTH = Q_LORA + KV_LORA + ROPE_DIM + 3 * BAND_WIDTH
IN_SPLITS = (Q_LORA, Q_LORA + KV_LORA, Q_LORA + KV_LORA + ROPE_DIM,
             Q_LORA + KV_LORA + ROPE_DIM + BAND_WIDTH, Q_LORA + KV_LORA + ROPE_DIM + 2 * BAND_WIDTH)
N_EXPERTS = 16
N_GROUPS = 4
GROUP_SIZE = N_EXPERTS // N_GROUPS
TOP_K = 2
EXPERT_FF = 512
EXPERT_BLOCK = 128
ALPHA = (2 * DEPTH) ** 0.25
BETA = (8 * DEPTH) ** -0.25
RMS_EPS = 1e-6
LN_EPS = 1e-5
NEG = -1e30

kernel_name = 'hymba_mla_chunkband_sharedrouter_moe_deepnorm_step'


def rms_norm(x, g):
    xf = x.astype(jnp.float32)
    y = xf * lax.rsqrt(jnp.mean(jnp.square(xf), -1, keepdims=True) + RMS_EPS)
    return (y * g.astype(jnp.float32)).astype(x.dtype)


def layer_norm(x, g, b):
    xf = x.astype(jnp.float32)
    mu = jnp.mean(xf, -1, keepdims=True)
    var = jnp.mean(jnp.square(xf - mu), -1, keepdims=True)
    y = (xf - mu) * lax.rsqrt(var + LN_EPS) * g.astype(jnp.float32) + b.astype(jnp.float32)
    return y.astype(x.dtype)


def rope_angles(pos, dtype):
    inv = 1.0 / (ROPE_THETA ** (jnp.arange(0, ROPE_DIM, 2, dtype=jnp.float32) / ROPE_DIM))
    ang = pos.astype(jnp.float32)[:, None] * inv[None, :]
    return jnp.cos(ang).astype(dtype), jnp.sin(ang).astype(dtype)


def apply_rope(x, cos, sin):
    half = ROPE_DIM // 2
    x1, x2 = x[..., :half], x[..., half:]
    return jnp.concatenate([x1 * cos - x2 * sin, x1 * sin + x2 * cos], -1)


def chunk_causal_attention(q, k, v, q_pos, k_pos):
    B, Tq, H, dk = q.shape
    scale = dk ** -0.5
    k_chunk = k_pos // CHUNK

    def block(qb, qpb):
        s = jnp.einsum('bqhd,bkhd->bhqk', qb, k).astype(jnp.float32) * scale
        mask = k_chunk[None, :] <= (qpb // CHUNK)[:, None]
        p = jax.nn.softmax(jnp.where(mask[None, None], s, NEG), axis=-1).astype(v.dtype)
        return jnp.einsum('bhqk,bkhd->bqhd', p, v)

    if Tq <= Q_BLOCK:
        return block(q, q_pos)
    nb = Tq // Q_BLOCK
    qb = q.reshape(B, nb, Q_BLOCK, H, dk).transpose(1, 0, 2, 3, 4)
    out = lax.map(lambda a: block(a[0], a[1]), (qb, q_pos.reshape(nb, Q_BLOCK)))
    return out.transpose(1, 0, 2, 3, 4).reshape(B, Tq, H, v.shape[-1])


def band_block(q, k, v, q_pos, k_pos, rel_bias):
    s = jnp.einsum('bqhd,bkhd->bhqk', q, k).astype(jnp.float32) * (q.shape[-1] ** -0.5)
    rel = jnp.clip(q_pos[:, None] - k_pos[None, :], -REL_CLIP, REL_CLIP) + REL_CLIP
    s = s + rel_bias[:, rel].astype(jnp.float32)[None]
    qc = (q_pos // CHUNK)[:, None]
    kc = (k_pos // CHUNK)[None, :]
    mask = (kc <= qc) & (kc >= qc - LEFT_CHUNKS) & (k_pos[None, :] >= 0)
    p = jax.nn.softmax(jnp.where(mask[None, None], s, NEG), axis=-1).astype(v.dtype)
    return jnp.einsum('bhqk,bkhd->bqhd', p, v)


def band_attention_prompt(q, k, v, rel_bias):
    B, S, H, d = q.shape
    nc = S // CHUNK
    padw = ((0, 0), (BAND_PAST, 0), (0, 0), (0, 0))
    kp, vp = jnp.pad(k, padw), jnp.pad(v, padw)
    kpos = jnp.arange(S + BAND_PAST, dtype=jnp.int32) - BAND_PAST
    qc = q.reshape(B, nc, CHUNK, H, d).transpose(1, 0, 2, 3, 4)

    def one_chunk(args):
        c, qb = args
        start = c * CHUNK
        kb = lax.dynamic_slice_in_dim(kp, start, BAND_LEN, axis=1)
        vb = lax.dynamic_slice_in_dim(vp, start, BAND_LEN, axis=1)
        kpb = lax.dynamic_slice_in_dim(kpos, start, BAND_LEN)
        qpos = start + jnp.arange(CHUNK, dtype=jnp.int32)
        return band_block(qb, kb, vb, qpos, kpb, rel_bias)

    out = lax.map(one_chunk, (jnp.arange(nc, dtype=jnp.int32), qc))
    return out.transpose(1, 0, 2, 3, 4).reshape(B, S, H, v.shape[-1])


def token_mixer(x, pos, past, w, l):
    B, T, _ = x.shape
    h = jnp.einsum('btd,de->bte', x, w['w_in'][l])
    c_q, c_kv, k_pe, q_b, k_b, v_b = jnp.split(h, IN_SPLITS, axis=-1)
    cos, sin = rope_angles(pos, x.dtype)
    q = jnp.einsum('btr,re->bte', rms_norm(c_q, w['g_q_lat'][l]), w['w_q_up'][l])
    q = q.reshape(B, T, H_MLA, NOPE_DIM + ROPE_DIM)
    q = jnp.concatenate([q[..., :NOPE_DIM], apply_rope(q[..., NOPE_DIM:], cos[:, None], sin[:, None])], -1)
    lat = rms_norm(c_kv, w['g_kv_lat'][l])
    k_pe = apply_rope(k_pe, cos, sin)
    k_b = k_b.reshape(B, T, H_BAND, D_BAND)
    v_b = v_b.reshape(B, T, H_BAND, D_BAND)
    q_b = q_b.reshape(B, T, H_BAND, D_BAND)
    if past is None:
        lat_all, rope_all, k_pos = lat, k_pe, pos
    else:
        c_lat, c_rope, c_bk, c_bv = past
        P = c_lat.shape[1]
        lat_all = jnp.concatenate([c_lat, lat], 1)
        rope_all = jnp.concatenate([c_rope, k_pe], 1)
        k_pos = jnp.concatenate([jnp.arange(P, dtype=jnp.int32), pos])
    Tk = lat_all.shape[1]
    k_nope = jnp.einsum('bkr,re->bke', lat_all, w['w_k_up'][l]).reshape(B, Tk, H_MLA, NOPE_DIM)
    v_a = jnp.einsum('bkr,re->bke', lat_all, w['w_v_up'][l]).reshape(B, Tk, H_MLA, V_DIM)
    k_full = jnp.concatenate([k_nope, jnp.broadcast_to(rope_all[:, :, None, :], (B, Tk, H_MLA, ROPE_DIM))], -1)
    o_a = chunk_causal_attention(q, k_full, v_a, pos, k_pos).reshape(B, T, MLA_WIDTH)
    if past is None:
        o_b = band_attention_prompt(q_b, k_b, v_b, w['rel_bias'][l])
        band_state_k, band_state_v = k_b[:, -BAND_PAST:], v_b[:, -BAND_PAST:]
    else:
        R = c_bk.shape[1]
        kb_all = jnp.concatenate([c_bk, k_b], 1)
        vb_all = jnp.concatenate([c_bv, v_b], 1)
        kb_pos = jnp.concatenate([P - R + jnp.arange(R, dtype=jnp.int32), pos])
        o_b = band_block(q_b, kb_all, vb_all, pos, kb_pos, w['rel_bias'][l])
        band_state_k, band_state_v = k_b, v_b
    o_b = o_b.reshape(B, T, BAND_WIDTH)
    o = jnp.concatenate([rms_norm(o_a, w['g_out_mla'][l]), rms_norm(o_b, w['g_out_band'][l])], -1)
    y = jnp.einsum('bte,ed->btd', o, w['w_out'][l])
    return y, (lat, k_pe, band_state_k, band_state_v)


def route(xf, w_router, b_router):
    s = jax.nn.sigmoid(jnp.einsum('nd,de->ne', xf, w_router).astype(jnp.float32))
    sel = (s + b_router.astype(jnp.float32)).reshape(-1, N_GROUPS, GROUP_SIZE)
    group_score = jnp.sum(lax.top_k(sel, 2)[0], -1)
    g = jnp.argmax(group_score, axis=-1)
    in_group = jnp.take_along_axis(sel, g[:, None, None], axis=1)[:, 0]
    _, local = lax.top_k(in_group, TOP_K)
    e_idx = (g[:, None] * GROUP_SIZE + local).astype(jnp.int32)
    gw = jnp.take_along_axis(s, e_idx, axis=1)
    return e_idx, gw / jnp.sum(gw, -1, keepdims=True)


def moe(x, w_router, b_router, w_gate, w_up, w_down):
    B, T, D = x.shape
    N = B * T
    xf = x.reshape(N, D)
    e_idx, gate = route(xf, w_router, b_router)
    A = N * TOP_K
    e_flat = e_idx.reshape(A)
    order = jnp.argsort(e_flat)
    e_sorted = e_flat[order]
    tok_sorted = (order // TOP_K).astype(jnp.int32)
    gate_sorted = gate.reshape(A)[order]
    counts = jnp.bincount(e_flat, length=N_EXPERTS)
    padded = (counts + EXPERT_BLOCK - 1) // EXPERT_BLOCK * EXPERT_BLOCK
    ends_pad = jnp.cumsum(padded)
    starts_pad = ends_pad - padded
    starts = jnp.cumsum(counts) - counts
    dest = starts_pad[e_sorted] + jnp.arange(A, dtype=jnp.int32) - starts[e_sorted]
    n_blocks = -(-A // EXPERT_BLOCK) + N_EXPERTS
    slot_tok = jnp.full((n_blocks * EXPERT_BLOCK,), N, jnp.int32).at[dest].set(tok_sorted)
    x_pad = jnp.concatenate([xf, jnp.zeros((1, D), xf.dtype)], 0)
    xb = x_pad[slot_tok].reshape(n_blocks, EXPERT_BLOCK, D)
    block_expert = jnp.minimum(
        jnp.searchsorted(ends_pad, jnp.arange(n_blocks, dtype=jnp.int32) * EXPERT_BLOCK, side='right'),
        N_EXPERTS - 1)

    def expert_block(args):
        xe, e = args
        hid = jax.nn.silu(xe @ w_gate[e]) * (xe @ w_up[e])
        return hid @ w_down[e]

    yb = lax.map(expert_block, (xb, block_expert)).reshape(n_blocks * EXPERT_BLOCK, D)
    contrib = yb[dest] * gate_sorted[:, None].astype(x.dtype)
    y = jnp.zeros((N, D), x.dtype).at[tok_sorted].add(contrib)
    return y.reshape(B, T, D)


def run_trunk(x, pos, caches, w):
    new = [[], [], [], []]
    for l in range(DEPTH):
        past = None if caches is None else (caches[0][l], caches[1][l], caches[2][l], caches[3][l])
        mix, st = token_mixer(x, pos, past, w, l)
        x = layer_norm(ALPHA * x + mix, w['ln1_g'][l], w['ln1_b'][l])
        ffn = moe(x, w['w_router'], w['b_router'], w['w_gate'][l], w['w_up'][l], w['w_down'][l])
        x = layer_norm(ALPHA * x + ffn, w['ln2_g'][l], w['ln2_b'][l])
        for acc, s in zip(new, st):
            acc.append(s)
    return x, [jnp.stack(a) for a in new]


def setup_inputs(seed: int = 0) -> dict:
    key = jax.random.key(seed)
    ks = jax.random.split(key, 32)
    nrm = lambda k, shape, scale=1.0: jax.random.normal(k, shape, jnp.float32) * scale
    R = min(BAND_PAST, PAST_LEN)
    return {
        'x_prompt': nrm(ks[0], (BATCH, SEQ, D_MODEL)),
        'x_sample': nrm(ks[1], (DEC_BATCH, DEC_SEQ, D_MODEL)),
        'cache_mla_latent': nrm(ks[2], (DEPTH, DEC_BATCH, PAST_LEN, KV_LORA)),
        'cache_mla_rope': nrm(ks[3], (DEPTH, DEC_BATCH, PAST_LEN, ROPE_DIM)),
        'cache_band_k': nrm(ks[4], (DEPTH, DEC_BATCH, R, H_BAND, D_BAND)),
        'cache_band_v': nrm(ks[5], (DEPTH, DEC_BATCH, R, H_BAND, D_BAND)),
        'w_in': nrm(ks[6], (DEPTH, D_MODEL, IN_WIDTH), D_MODEL ** -0.5),
        'g_q_lat': 1.0 + nrm(ks[7], (DEPTH, Q_LORA), 0.02),
        'w_q_up': nrm(ks[8], (DEPTH, Q_LORA, H_MLA * (NOPE_DIM + ROPE_DIM)), Q_LORA ** -0.5),
        'g_kv_lat': 1.0 + nrm(ks[9], (DEPTH, KV_LORA), 0.02),
        'w_k_up': nrm(ks[10], (DEPTH, KV_LORA, H_MLA * NOPE_DIM), KV_LORA ** -0.5),
        'w_v_up': nrm(ks[11], (DEPTH, KV_LORA, H_MLA * V_DIM), KV_LORA ** -0.5),
        'rel_bias': nrm(ks[12], (DEPTH, H_BAND, 2 * REL_CLIP + 1), 0.1),
        'g_out_mla': 1.0 + nrm(ks[13], (DEPTH, MLA_WIDTH), 0.02),
        'g_out_band': 1.0 + nrm(ks[14], (DEPTH, BAND_WIDTH), 0.02),
        'w_out': nrm(ks[15], (DEPTH, MIX_WIDTH, D_MODEL), BETA * MIX_WIDTH ** -0.5),
        'ln1_g': 1.0 + nrm(ks[16], (DEPTH, D_MODEL), 0.02),
        'ln1_b': nrm(ks[17], (DEPTH, D_MODEL), 0.01),
        'w_router': nrm(ks[18], (D_MODEL, N_EXPERTS), D_MODEL ** -0.5),
        'b_router': nrm(ks[19], (N_EXPERTS,), 0.01),
        'w_gate': nrm(ks[20], (DEPTH, N_EXPERTS, D_MODEL, EXPERT_FF), D_MODEL ** -0.5),
        'w_up': nrm(ks[21], (DEPTH, N_EXPERTS, D_MODEL, EXPERT_FF), D_MODEL ** -0.5),
        'w_down': nrm(ks[22], (DEPTH, N_EXPERTS, EXPERT_FF, D_MODEL), BETA * EXPERT_FF ** -0.5),
        'ln2_g': 1.0 + nrm(ks[23], (DEPTH, D_MODEL), 0.02),
        'ln2_b': nrm(ks[24], (DEPTH, D_MODEL), 0.01),
    }


def reference(x_prompt, x_sample, cache_mla_latent, cache_mla_rope, cache_band_k, cache_band_v,
              w_in, g_q_lat, w_q_up, g_kv_lat, w_k_up, w_v_up, rel_bias, g_out_mla, g_out_band, w_out,
              ln1_g, ln1_b, w_router, b_router, w_gate, w_up, w_down, ln2_g, ln2_b):
    w = dict(w_in=w_in, g_q_lat=g_q_lat, w_q_up=w_q_up, g_kv_lat=g_kv_lat, w_k_up=w_k_up, w_v_up=w_v_up,
             rel_bias=rel_bias, g_out_mla=g_out_mla, g_out_band=g_out_band, w_out=w_out,
             ln1_g=ln1_g, ln1_b=ln1_b, w_router=w_router, b_router=b_router,
             w_gate=w_gate, w_up=w_up, w_down=w_down, ln2_g=ln2_g, ln2_b=ln2_b)
    pos_p = jnp.arange(x_prompt.shape[1], dtype=jnp.int32)
    y_prompt, p_states = run_trunk(x_prompt, pos_p, None, w)
    p_lat, p_rope, p_band_k, p_band_v = p_states
    past_len = cache_mla_latent.shape[2]
    pos_s = past_len + jnp.arange(x_sample.shape[1], dtype=jnp.int32)
    y_sample, s_states = run_trunk(x_sample, pos_s,
                                   (cache_mla_latent, cache_mla_rope, cache_band_k, cache_band_v), w)
    s_lat, s_rope, s_band_k, s_band_v = s_states
    return (y_prompt, y_sample, p_lat, p_rope, p_band_k, p_band_v, s_lat, s_rope, s_band_k, s_band_v)
```

```python
import functools

import jax
import jax.numpy as jnp
from jax import lax
from jax.experimental import pallas as pl
from jax.experimental.pallas import tpu as pltpu

D_MODEL = 2048
CHUNK = 64
H_MLA = 8
NOPE_DIM = 128
ROPE_DIM = 64
HALF_ROPE = ROPE_DIM // 2
V_DIM = 128
Q_LORA = 512
KV_LORA = 256
ROPE_THETA = 10000.0
H_BAND = 8
D_BAND = 128
LEFT_CHUNKS = 8
BAND_PAST = LEFT_CHUNKS * CHUNK
REL_CLIP = 128
MLA_WIDTH = H_MLA * V_DIM
BAND_WIDTH = H_BAND * D_BAND
QK_PAD = 256
N_EXPERTS = 16
N_GROUPS = 4
GROUP_SIZE = 4
EXPERT_FF = 512
RMS_EPS = 1e-6
LN_EPS = 1e-5
NEG = -1e30
LANE = 128
VMEM_LIMIT = 56 * 1024 * 1024

F32 = jnp.float32
BF16 = jnp.bfloat16
HI = lax.Precision.HIGHEST


def _cparams(sem):
    return pltpu.CompilerParams(dimension_semantics=sem, vmem_limit_bytes=VMEM_LIMIT)


def _dot(a, b, precise=False):
    return jnp.dot(a, b, preferred_element_type=F32, precision=HI if precise else None)


def _dot_nt(a, b, precise=False):
    return lax.dot_general(a, b, (((1,), (1,)), ((), ())), preferred_element_type=F32,
                           precision=HI if precise else None)


def _rms(x, g):
    return x * lax.rsqrt(jnp.mean(x * x, axis=-1, keepdims=True) + RMS_EPS) * g


def _layer_norm(x, g, b):
    mu = jnp.mean(x, axis=-1, keepdims=True)
    xc = x - mu
    var = jnp.mean(xc * xc, axis=-1, keepdims=True)
    return xc * lax.rsqrt(var + LN_EPS) * g + b


def _swap32(x):
    w = x.shape[-1]
    lane = lax.broadcasted_iota(jnp.int32, x.shape, x.ndim - 1)
    return jnp.where((lane % ROPE_DIM) < HALF_ROPE, pltpu.roll(x, w - HALF_ROPE, x.ndim - 1),
                     pltpu.roll(x, HALF_ROPE, x.ndim - 1))


def _mla_proj_kernel(x_ref, wa_ref, gq_ref, wq_ref, gkv_ref, wk_ref, wv_ref, c_ref, s_ref,
                     q_ref, k_ref, v_ref, lat_ref, kpe_ref):
    xb = x_ref[...].astype(BF16)
    h = _dot(xb, wa_ref[...])
    cqn = _rms(h[:, :Q_LORA], gq_ref[...])
    lat = _rms(h[:, Q_LORA:Q_LORA + KV_LORA], gkv_ref[...])
    kp = h[:, Q_LORA + KV_LORA:]
    cos = c_ref[...]
    sin = s_ref[...]
    kpr = kp * cos + pltpu.roll(kp, 2 * HALF_ROPE, 1) * sin
    lat_ref[...] = lat
    kpe_ref[...] = (kpr + pltpu.roll(kpr, LANE - HALF_ROPE, 1))[:, :ROPE_DIM]
    kprb = kpr.astype(BF16)
    latb = lat.astype(BF16)
    q = _dot(cqn.astype(BF16), wq_ref[...])
    kn = _dot(latb, wk_ref[...])
    v_ref[...] = _dot(latb, wv_ref[...]).astype(BF16)
    for hd in range(H_MLA):
        a = hd * QK_PAD
        qv = q[:, a + NOPE_DIM:a + QK_PAD]
        q_ref[:, a:a + NOPE_DIM] = q[:, a:a + NOPE_DIM].astype(BF16)
        q_ref[:, a + NOPE_DIM:a + QK_PAD] = (qv * cos + pltpu.roll(qv, 2 * HALF_ROPE, 1) * sin).astype(BF16)
        k_ref[:, a:a + NOPE_DIM] = kn[:, a:a + NOPE_DIM].astype(BF16)
        k_ref[:, a + NOPE_DIM:a + QK_PAD] = kprb


def _mla_proj(x, wa, gq, wq, gkv, wk, wv, cos, sin, seq, tm):
    n = x.shape[0]
    tpb = seq // tm
    const = lambda i: (0, 0)
    row = lambda i: (i, 0)
    return pl.pallas_call(
        _mla_proj_kernel,
        grid=(n // tm,),
        in_specs=[
            pl.BlockSpec((tm, D_MODEL), row),
            pl.BlockSpec(wa.shape, const),
            pl.BlockSpec(gq.shape, const),
            pl.BlockSpec(wq.shape, const),
            pl.BlockSpec(gkv.shape, const),
            pl.BlockSpec(wk.shape, const),
            pl.BlockSpec(wv.shape, const),
            pl.BlockSpec((tm, LANE), lambda i: (i % tpb, 0)),
            pl.BlockSpec((tm, LANE), lambda i: (i % tpb, 0)),
        ],
        out_specs=[
            pl.BlockSpec((tm, H_MLA * QK_PAD), row),
            pl.BlockSpec((tm, H_MLA * QK_PAD), row),
            pl.BlockSpec((tm, MLA_WIDTH), row),
            pl.BlockSpec((tm, KV_LORA), row),
            pl.BlockSpec((tm, ROPE_DIM), row),
        ],
        out_shape=[
            jax.ShapeDtypeStruct((n, H_MLA * QK_PAD), BF16),
            jax.ShapeDtypeStruct((n, H_MLA * QK_PAD), BF16),
            jax.ShapeDtypeStruct((n, MLA_WIDTH), BF16),
            jax.ShapeDtypeStruct((n, KV_LORA), F32),
            jax.ShapeDtypeStruct((n, ROPE_DIM), F32),
        ],
        compiler_params=_cparams(("parallel",)),
        name="mla_proj",
    )(x, wa, gq, wq, gkv, wk, wv, cos, sin)


def _band_proj_kernel(x_ref, w_ref, q_ref, k_ref, v_ref, ks_ref, vs_ref, *, tpb, ntail):
    xb = x_ref[...].astype(BF16)
    q_ref[...] = _dot(xb, w_ref[:, :BAND_WIDTH]).astype(BF16)
    k = _dot(xb, w_ref[:, BAND_WIDTH:2 * BAND_WIDTH])
    v = _dot(xb, w_ref[:, 2 * BAND_WIDTH:])
    k_ref[...] = k.astype(BF16)
    v_ref[...] = v.astype(BF16)

    @pl.when(pl.program_id(0) % tpb >= tpb - ntail)
    def _():
        ks_ref[...] = k
        vs_ref[...] = v


def _band_proj(x, w, seq, tm):
    n = x.shape[0]
    tpb = seq // tm
    ntail = BAND_PAST // tm
    batch = n // seq
    row = lambda i: (i, 0)
    tail = lambda i: (i // tpb, jnp.maximum(i % tpb - (tpb - ntail), 0), 0)
    return pl.pallas_call(
        functools.partial(_band_proj_kernel, tpb=tpb, ntail=ntail),
        grid=(n // tm,),
        in_specs=[pl.BlockSpec((tm, D_MODEL), row), pl.BlockSpec(w.shape, lambda i: (0, 0))],
        out_specs=[
            pl.BlockSpec((tm, BAND_WIDTH), row),
            pl.BlockSpec((tm, BAND_WIDTH), row),
            pl.BlockSpec((tm, BAND_WIDTH), row),
            pl.BlockSpec((None, tm, BAND_WIDTH), tail),
            pl.BlockSpec((None, tm, BAND_WIDTH), tail),
        ],
        out_shape=[
            jax.ShapeDtypeStruct((n, BAND_WIDTH), BF16),
            jax.ShapeDtypeStruct((n, BAND_WIDTH), BF16),
            jax.ShapeDtypeStruct((n, BAND_WIDTH), BF16),
            jax.ShapeDtypeStruct((batch, BAND_PAST, BAND_WIDTH), F32),
            jax.ShapeDtypeStruct((batch, BAND_PAST, BAND_WIDTH), F32),
        ],
        compiler_params=_cparams(("arbitrary",)),
        name="band_proj",
    )(x, w)


def _mla_attn_kernel(q_ref, k_ref, v_ref, o_ref, *, tq):
    qi = pl.program_id(2)
    q = q_ref[...]

    def step(j, carry, masked):
        m, l, acc = carry
        off = pl.multiple_of(j * tq, tq)
        k = k_ref[pl.ds(off, tq), :]
        v = v_ref[pl.ds(off, tq), :]
        s = _dot_nt(q, k)
        if masked:
            rc = lax.broadcasted_iota(jnp.int32, s.shape, 0) // CHUNK
            cc = lax.broadcasted_iota(jnp.int32, s.shape, 1) // CHUNK
            s = jnp.where(cc <= rc, s, NEG)
        m_new = jnp.maximum(m, jnp.max(s, axis=-1, keepdims=True))
        p = jnp.exp(s - m_new)
        alpha = jnp.exp(m - m_new)
        l = alpha * l + jnp.sum(p, axis=-1, keepdims=True)
        acc = alpha * acc + _dot(p.astype(BF16), v)
        return m_new, l, acc

    init = (jnp.full((tq, 1), NEG, F32), jnp.zeros((tq, 1), F32), jnp.zeros((tq, V_DIM), F32))
    carry = lax.fori_loop(0, qi, lambda j, c: step(j, c, False), init)
    _, l, acc = step(qi, carry, True)
    o_ref[...] = (acc / l).astype(o_ref.dtype)


def _mla_attn(q, k, v, seq, tq):
    n = q.shape[0]
    batch = n // seq
    nq = seq // tq
    return pl.pallas_call(
        functools.partial(_mla_attn_kernel, tq=tq),
        grid=(batch, H_MLA, nq),
        in_specs=[
            pl.BlockSpec((tq, QK_PAD), lambda b, h, i: (b * nq + i, h)),
            pl.BlockSpec((seq, QK_PAD), lambda b, h, i: (b, h)),
            pl.BlockSpec((seq, V_DIM), lambda b, h, i: (b, h)),
        ],
        out_specs=pl.BlockSpec((tq, V_DIM), lambda b, h, i: (b * nq + i, h)),
        out_shape=jax.ShapeDtypeStruct((n, MLA_WIDTH), BF16),
        compiler_params=_cparams(("parallel", "parallel", "arbitrary")),
        name="mla_attn",
    )(q, k, v)


def _band_attn_kernel(q_ref, kp_ref, kc_ref, vp_ref, vc_ref, b_ref, o_ref):
    qi = pl.program_id(2)
    q = q_ref[...]
    sp = _dot_nt(q, kp_ref[...]) + b_ref[:, :BAND_PAST]
    sp = jnp.where(qi > 0, sp, NEG)
    sc = _dot_nt(q, kc_ref[...]) + b_ref[:, BAND_PAST:]
    m = jnp.maximum(jnp.max(sp, axis=-1, keepdims=True), jnp.max(sc, axis=-1, keepdims=True))
    pp = jnp.exp(sp - m)
    pc = jnp.exp(sc - m)
    l = jnp.sum(pp, axis=-1, keepdims=True) + jnp.sum(pc, axis=-1, keepdims=True)
    o = _dot(pp.astype(BF16), vp_ref[...]) + _dot(pc.astype(BF16), vc_ref[...])
    o_ref[...] = (o / l).astype(o_ref.dtype)


def _band_attn(q, k, v, bias, seq):
    n = q.shape[0]
    tq = BAND_PAST
    batch = n // seq
    nq = seq // tq
    cur = lambda b, h, i: (b * nq + i, h)
    prev = lambda b, h, i: (b * nq + jnp.maximum(i - 1, 0), h)
    return pl.pallas_call(
        _band_attn_kernel,
        grid=(batch, H_BAND, nq),
        in_specs=[
            pl.BlockSpec((tq, D_BAND), cur),
            pl.BlockSpec((tq, D_BAND), prev),
            pl.BlockSpec((tq, D_BAND), cur),
            pl.BlockSpec((tq, D_BAND), prev),
            pl.BlockSpec((tq, D_BAND), cur),
            pl.BlockSpec((None, tq, 2 * tq), lambda b, h, i: (h, 0, 0)),
        ],
        out_specs=pl.BlockSpec((tq, D_BAND), cur),
        out_shape=jax.ShapeDtypeStruct((n, BAND_WIDTH), BF16),
        compiler_params=_cparams(("parallel", "parallel", "arbitrary")),
        name="band_attn",
    )(q, k, k, v, v, bias)


def _top2_of4(v, sv):
    best, bi, bs = v[0], jnp.zeros(v[0].shape, jnp.int32), sv[0]
    for i in range(1, GROUP_SIZE):
        c = v[i] > best
        best = jnp.where(c, v[i], best)
        bi = jnp.where(c, i, bi)
        bs = jnp.where(c, sv[i], bs)
    sec = jnp.full(v[0].shape, -jnp.inf, F32)
    si = jnp.zeros(v[0].shape, jnp.int32)
    ss = jnp.zeros(v[0].shape, F32)
    for i in range(GROUP_SIZE):
        c = (bi != i) & (v[i] > sec)
        sec = jnp.where(c, v[i], sec)
        si = jnp.where(c, i, si)
        ss = jnp.where(c, sv[i], ss)
    return best + sec, bi, si, bs, ss


def _route(sel, s):
    groups = []
    for g in range(N_GROUPS):
        rows = range(g * GROUP_SIZE, (g + 1) * GROUP_SIZE)
        groups.append(_top2_of4([sel[i:i + 1, :] for i in rows], [s[i:i + 1, :] for i in rows]))
    score, bi, si, bs, ss = groups[0]
    gi = jnp.zeros(score.shape, jnp.int32)
    for g in range(1, N_GROUPS):
        c = groups[g][0] > score
        score = jnp.where(c, groups[g][0], score)
        gi = jnp.where(c, g, gi)
        bi = jnp.where(c, groups[g][1], bi)
        si = jnp.where(c, groups[g][2], si)
        bs = jnp.where(c, groups[g][3], bs)
        ss = jnp.where(c, groups[g][4], ss)
    den = bs + ss
    return gi * GROUP_SIZE + bi, gi * GROUP_SIZE + si, bs / den, ss / den


def _out_route_kernel(oa_ref, ob_ref, ga_ref, gb_ref, wo_ref, x_ref, lg_ref, lb_ref, wr_ref, br_ref, u_ref,
                      x1_ref, e_ref, r_ref, g_ref, cnt_ref, base_ref, *, alpha, precise):
    @pl.when(pl.program_id(0) == 0)
    def _():
        base_ref[...] = jnp.zeros_like(base_ref)

    mm = wo_ref.dtype
    na = _rms(oa_ref[...].astype(F32), ga_ref[...]).astype(mm)
    nb = _rms(ob_ref[...].astype(F32), gb_ref[...]).astype(mm)
    mix = _dot(na, wo_ref[:MLA_WIDTH, :], precise) + _dot(nb, wo_ref[MLA_WIDTH:, :], precise)
    x1 = _layer_norm(alpha * x_ref[...] + mix, lg_ref[...], lb_ref[...])
    x1_ref[...] = x1
    logits = _dot_nt(wr_ref[...], x1.astype(mm), precise)
    s = 1.0 / (1.0 + jnp.exp(-logits))
    e0, e1, g0, g1 = _route(s + br_ref[...], s)
    e_ref[0:1, :] = e0
    e_ref[1:2, :] = e1
    g_ref[0:1, :] = g0
    g_ref[1:2, :] = g1
    eid = lax.broadcasted_iota(jnp.int32, logits.shape, 0)
    oh0 = eid == e0
    oh1 = eid == e1
    both = jnp.where(oh0 | oh1, 1.0, 0.0)
    before = _dot(both.astype(BF16), u_ref[...]) + base_ref[...]
    r_ref[0:1, :] = jnp.sum(jnp.where(oh0, before, 0.0), axis=0, keepdims=True).astype(jnp.int32)
    r_ref[1:2, :] = jnp.sum(jnp.where(oh1, before, 0.0), axis=0, keepdims=True).astype(jnp.int32)
    base = base_ref[...] + jnp.sum(both, axis=1, keepdims=True)
    base_ref[...] = base
    cnt_ref[...] = base


def _out_route(oa, ob, ga, gb, wo, x, lg, lb, wr, br, alpha, tm, precise):
    n = x.shape[0]
    u = (lax.broadcasted_iota(jnp.int32, (tm, tm), 0) < lax.broadcasted_iota(jnp.int32, (tm, tm), 1)).astype(BF16)
    const = lambda i: (0, 0)
    row = lambda i: (i, 0)
    col = lambda i: (0, i)
    return pl.pallas_call(
        functools.partial(_out_route_kernel, alpha=alpha, precise=precise),
        grid=(n // tm,),
        in_specs=[
            pl.BlockSpec((tm, MLA_WIDTH), row),
            pl.BlockSpec((tm, BAND_WIDTH), row),
            pl.BlockSpec(ga.shape, const),
            pl.BlockSpec(gb.shape, const),
            pl.BlockSpec(wo.shape, const),
            pl.BlockSpec((tm, D_MODEL), row),
            pl.BlockSpec(lg.shape, const),
            pl.BlockSpec(lb.shape, const),
            pl.BlockSpec(wr.shape, const),
            pl.BlockSpec(br.shape, const),
            pl.BlockSpec((tm, tm), const),
        ],
        out_specs=[
            pl.BlockSpec((tm, D_MODEL), row),
            pl.BlockSpec((2, tm), col),
            pl.BlockSpec((2, tm), col),
            pl.BlockSpec((2, tm), col),
            pl.BlockSpec((N_EXPERTS, 1), const),
        ],
        out_shape=[
            jax.ShapeDtypeStruct((n, D_MODEL), F32),
            jax.ShapeDtypeStruct((2, n), jnp.int32),
            jax.ShapeDtypeStruct((2, n), jnp.int32),
            jax.ShapeDtypeStruct((2, n), F32),
            jax.ShapeDtypeStruct((N_EXPERTS, 1), F32),
        ],
        scratch_shapes=[pltpu.VMEM((N_EXPERTS, 1), F32)],
        compiler_params=_cparams(("arbitrary",)),
        name="out_route",
    )(oa, ob, ga, gb, wo, x, lg, lb, wr, br, u)


def _slots_kernel(cnt_ref, e_ref, r_ref, d_ref, blk_ref, ex_ref, lo_ref, hi_ref, *, tb, nblocks):
    shift = tb.bit_length() - 1
    e = e_ref[...]
    dest = r_ref[...]
    item = lax.broadcasted_iota(jnp.int32, blk_ref.shape, 1)
    blk = jnp.zeros(blk_ref.shape, jnp.int32)
    exv = jnp.zeros(blk_ref.shape, jnp.int32)
    lo = jnp.zeros(blk_ref.shape, jnp.int32)
    hi = jnp.zeros(blk_ref.shape, jnp.int32)
    start = jnp.int32(0)
    first = jnp.int32(0)
    last_e = jnp.int32(0)
    for ex in range(N_EXPERTS):
        cnt = cnt_ref[ex]
        end = start + cnt
        sb = lax.shift_right_logical(start, shift)
        nitems = jnp.where(cnt > 0, lax.shift_right_logical(end - 1, shift) - sb + 1, 0)
        dest = dest + jnp.where(e == ex, start, 0)
        mine = (item >= first) & (item < first + nitems)
        b = sb + item - first
        blk = jnp.where(mine, b, blk)
        exv = jnp.where(mine, ex, exv)
        lo = jnp.where(mine, jnp.maximum(start, b * tb) - b * tb, lo)
        hi = jnp.where(mine, jnp.minimum(end, (b + 1) * tb) - b * tb, hi)
        last_e = jnp.where(cnt > 0, ex, last_e)
        start = end
        first = first + nitems
    unused = item >= first
    d_ref[...] = dest
    blk_ref[...] = jnp.where(unused, nblocks - 1, blk)
    ex_ref[...] = jnp.where(unused, last_e, exv)
    lo_ref[...] = lo
    hi_ref[...] = hi


def _slots(cnt, e, r, tb, nwp):
    n = e.shape[1]
    full = lambda shape: pl.BlockSpec(shape, lambda: (0, 0))
    return pl.pallas_call(
        functools.partial(_slots_kernel, tb=tb, nblocks=2 * n // tb),
        in_specs=[pl.BlockSpec(memory_space=pltpu.SMEM), full((2, n)), full((2, n))],
        out_specs=[full((2, n))] + [full((1, nwp))] * 4,
        out_shape=[jax.ShapeDtypeStruct((2, n), jnp.int32)] + [jax.ShapeDtypeStruct((1, nwp), jnp.int32)] * 4,
        name="moe_slots",
    )(cnt, e, r)


def _dispatch_kernel(d_ref, x_ref, xs_ref, sem, *, tm):
    def issue(r, c):
        for k in range(2):
            pltpu.make_async_copy(x_ref.at[pl.ds(r, 1)], xs_ref.at[pl.ds(d_ref[0, 0, k * tm + r], 1)], sem).start()
        return c

    lax.fori_loop(0, tm, issue, 0)

    def drain(r, c):
        for k in range(2):
            pltpu.make_async_copy(x_ref.at[pl.ds(0, 1)], xs_ref.at[pl.ds(0, 1)], sem).wait()
        return c

    lax.fori_loop(0, tm, drain, 0)


def _dispatch(dest3, x1, nslots, tm):
    n = x1.shape[0]
    return pl.pallas_call(
        functools.partial(_dispatch_kernel, tm=tm),
        grid=(n // tm,),
        in_specs=[
            pl.BlockSpec((1, 1, 2 * tm), lambda i: (i, 0, 0), memory_space=pltpu.SMEM),
            pl.BlockSpec((tm, D_MODEL), lambda i: (i, 0)),
        ],
        out_specs=pl.BlockSpec(memory_space=pl.ANY),
        out_shape=jax.ShapeDtypeStruct((nslots, D_MODEL), F32),
        scratch_shapes=[pltpu.SemaphoreType.DMA(())],
        compiler_params=_cparams(("arbitrary",)),
        name="moe_dispatch",
    )(dest3, x1)


def _experts_kernel(blk_ref, ex_ref, lo_ref, hi_ref, xs_ref, wg_ref, wu_ref, wd_ref, y_ref):
    i = pl.program_id(0)
    lo = lo_ref[i]
    hi = hi_ref[i]
    opens_block = (i == 0) | (blk_ref[i] != blk_ref[jnp.maximum(i - 1, 0)])

    @pl.when(lo < hi)
    def _():
        xb = xs_ref[...].astype(BF16)
        g = _dot(xb, wg_ref[...])
        u = _dot(xb, wu_ref[...])
        hid = (g / (1.0 + jnp.exp(-g))) * u
        y = _dot(hid.astype(BF16), wd_ref[...])
        row = lax.broadcasted_iota(jnp.int32, (y.shape[0], 1), 0)
        y = jnp.where((row >= lo) & (row < hi), y, 0.0)

        @pl.when(opens_block)
        def _():
            y_ref[...] = y

        @pl.when(jnp.logical_not(opens_block))
        def _():
            y_ref[...] += y


def _experts(blk, ex, lo, hi, xs, wg, wu, wd, layer, tb, nwork):
    nslots = xs.shape[0]
    rows = lambda i, blk, ex, lo, hi: (blk[i], 0)
    wmap = lambda i, blk, ex, lo, hi: (layer, ex[i], 0, 0)
    return pl.pallas_call(
        _experts_kernel,
        grid_spec=pltpu.PrefetchScalarGridSpec(
            num_scalar_prefetch=4,
            grid=(nwork,),
            in_specs=[
                pl.BlockSpec((tb, D_MODEL), rows),
                pl.BlockSpec((None, None, D_MODEL, EXPERT_FF), wmap),
                pl.BlockSpec((None, None, D_MODEL, EXPERT_FF), wmap),
                pl.BlockSpec((None, None, EXPERT_FF, D_MODEL), wmap),
            ],
            out_specs=pl.BlockSpec((tb, D_MODEL), rows),
        ),
        out_shape=jax.ShapeDtypeStruct((nslots, D_MODEL), F32),
        compiler_params=_cparams(("arbitrary",)),
        name="moe_experts",
    )(blk, ex, lo, hi, xs, wg, wu, wd)


def _combine_kernel(d_ref, x_ref, gt_ref, lg_ref, lb_ref, y_ref, o_ref, buf, sem, *, tm, alpha):
    def issue(r, c):
        for k in range(2):
            pltpu.make_async_copy(y_ref.at[pl.ds(d_ref[0, 0, k * tm + r], 1)], buf.at[k, pl.ds(r, 1)], sem).start()
        return c

    lax.fori_loop(0, tm, issue, 0)

    def drain(r, c):
        for k in range(2):
            pltpu.make_async_copy(y_ref.at[pl.ds(0, 1)], buf.at[0, pl.ds(0, 1)], sem).wait()
        return c

    lax.fori_loop(0, tm, drain, 0)
    ffn = buf[0] * gt_ref[:, 0:1] + buf[1] * gt_ref[:, 1:2]
    o_ref[...] = _layer_norm(alpha * x_ref[...] + ffn, lg_ref[...], lb_ref[...])


def _combine(dest3, x1, gates_t, lg, lb, yb, alpha, tm):
    n = x1.shape[0]
    const = lambda i: (0, 0)
    row = lambda i: (i, 0)
    return pl.pallas_call(
        functools.partial(_combine_kernel, tm=tm, alpha=alpha),
        grid=(n // tm,),
        in_specs=[
            pl.BlockSpec((1, 1, 2 * tm), lambda i: (i, 0, 0), memory_space=pltpu.SMEM),
            pl.BlockSpec((tm, D_MODEL), row),
            pl.BlockSpec((tm, 2), row),
            pl.BlockSpec(lg.shape, const),
            pl.BlockSpec(lb.shape, const),
            pl.BlockSpec(memory_space=pl.ANY),
        ],
        out_specs=pl.BlockSpec((tm, D_MODEL), row),
        out_shape=jax.ShapeDtypeStruct((n, D_MODEL), F32),
        scratch_shapes=[pltpu.VMEM((2, tm, D_MODEL), F32), pltpu.SemaphoreType.DMA(())],
        compiler_params=_cparams(("arbitrary",)),
        name="moe_combine",
    )(dest3, x1, gates_t, lg, lb, yb)


def _moe_ln(x1, e, r, gates, cnt, wg, wu, wd, layer, lg, lb, alpha, tm, tb):
    n = x1.shape[0]
    nwork = (2 * n) // tb + N_EXPERTS - 1
    nwp = -(-nwork // LANE) * LANE
    dest, blk, ex, lo, hi = _slots(cnt.astype(jnp.int32).reshape(N_EXPERTS), e, r, tb, nwp)
    dest3 = dest.reshape(2, n // tm, tm).transpose(1, 0, 2).reshape(n // tm, 1, 2 * tm)
    xs = _dispatch(dest3, x1, 2 * n, tm)
    yb = _experts(blk.reshape(nwp), ex.reshape(nwp), lo.reshape(nwp), hi.reshape(nwp), xs, wg, wu, wd,
                  layer, tb, nwork)
    return _combine(dest3, x1, gates.T, lg, lb, yb, alpha, tm)


def _sample_proj_kernel(x_ref, w_ref, gq_ref, wq_ref, gkv_ref, c_ref, s_ref,
                        q_ref, lat_ref, kpe_ref, qb_ref, kb_ref, vb_ref, acc_ref, *, band_scale):
    k = pl.program_id(0)

    @pl.when(k == 0)
    def _():
        acc_ref[...] = jnp.zeros_like(acc_ref)

    acc_ref[...] += _dot(x_ref[...], w_ref[...], True)

    @pl.when(k == pl.num_programs(0) - 1)
    def _():
        cos = c_ref[...]
        sin = s_ref[...]
        cqn = _rms(acc_ref[:, :Q_LORA], gq_ref[...])
        q = _dot(cqn, wq_ref[...], True)
        q_ref[:, :H_MLA * NOPE_DIM] = q[:, :H_MLA * NOPE_DIM]
        qr = q[:, H_MLA * NOPE_DIM:]
        q_ref[:, H_MLA * NOPE_DIM:] = qr * cos + _swap32(qr) * sin
        lat_ref[...] = _rms(acc_ref[:, Q_LORA:Q_LORA + KV_LORA], gkv_ref[...])
        a = Q_LORA + KV_LORA
        kv = acc_ref[:, a:a + LANE]
        kpe_ref[...] = (kv * cos[:, :LANE] + _swap32(kv) * sin[:, :LANE])[:, :ROPE_DIM]
        a += ROPE_DIM
        qb_ref[...] = acc_ref[:, a:a + BAND_WIDTH] * band_scale
        kb_ref[...] = acc_ref[:, a + BAND_WIDTH:a + 2 * BAND_WIDTH]
        vb_ref[...] = acc_ref[:, a + 2 * BAND_WIDTH:a + 3 * BAND_WIDTH]


def _sample_proj(x, w_in, layer, gq, wq, gkv, cos, sin, tk):
    m = x.shape[0]
    width = w_in.shape[2]
    const = lambda k: (0, 0)
    shapes = [(m, wq.shape[1]), (m, KV_LORA), (m, ROPE_DIM), (m, BAND_WIDTH), (m, BAND_WIDTH), (m, BAND_WIDTH)]
    return pl.pallas_call(
        functools.partial(_sample_proj_kernel, band_scale=D_BAND ** -0.5),
        grid=(D_MODEL // tk,),
        in_specs=[
            pl.BlockSpec((m, tk), lambda k: (0, k)),
            pl.BlockSpec((None, tk, width), lambda k: (layer, k, 0)),
            pl.BlockSpec(gq.shape, const),
            pl.BlockSpec(wq.shape, const),
            pl.BlockSpec(gkv.shape, const),
            pl.BlockSpec(cos.shape, const),
            pl.BlockSpec(sin.shape, const),
        ],
        out_specs=[pl.BlockSpec(s, const) for s in shapes],
        out_shape=[jax.ShapeDtypeStruct(s, F32) for s in shapes],
        scratch_shapes=[pltpu.VMEM((m, width), F32)],
        compiler_params=_cparams(("arbitrary",)),
        name="sample_proj",
    )(x, w_in, gq, wq, gkv, cos, sin)


def _sample_mla_kernel(q_ref, lat_ref, kpe_ref, cl_ref, cr_ref, wk_ref, wv_ref, o_ref):
    lat_n = lat_ref[...]
    kpe_n = kpe_ref[...]
    lat_c = cl_ref[...]
    kpe_c = cr_ref[...]
    for hd in range(H_MLA):
        a = hd * NOPE_DIM
        qa = _dot_nt(q_ref[:, a:a + NOPE_DIM], wk_ref[:, a:a + NOPE_DIM], True)
        b = H_MLA * NOPE_DIM + hd * ROPE_DIM
        qr = q_ref[:, b:b + ROPE_DIM]
        sc = _dot_nt(qa, lat_c, True) + _dot_nt(qr, kpe_c, True)
        sn = _dot_nt(qa, lat_n, True) + _dot_nt(qr, kpe_n, True)
        m = jnp.maximum(jnp.max(sc, axis=-1, keepdims=True), jnp.max(sn, axis=-1, keepdims=True))
        pc = jnp.exp(sc - m)
        pn = jnp.exp(sn - m)
        l = jnp.sum(pc, axis=-1, keepdims=True) + jnp.sum(pn, axis=-1, keepdims=True)
        ctx = (_dot(pc, lat_c, True) + _dot(pn, lat_n, True)) / l
        o_ref[:, a:a + V_DIM] = _dot(ctx, wv_ref[:, a:a + V_DIM], True)


def _sample_mla(q, lat, kpe, cache_lat, cache_rope, wk, wv, layer, t):
    m = q.shape[0]
    batch = m // t
    past = cache_lat.shape[2]
    row = lambda b: (b, 0)
    const = lambda b: (0, 0)
    return pl.pallas_call(
        _sample_mla_kernel,
        grid=(batch,),
        in_specs=[
            pl.BlockSpec((t, q.shape[1]), row),
            pl.BlockSpec((t, KV_LORA), row),
            pl.BlockSpec((t, ROPE_DIM), row),
            pl.BlockSpec((None, None, past, KV_LORA), lambda b: (layer, b, 0, 0)),
            pl.BlockSpec((None, None, past, ROPE_DIM), lambda b: (layer, b, 0, 0)),
            pl.BlockSpec((None, KV_LORA, MLA_WIDTH), lambda b: (layer, 0, 0)),
            pl.BlockSpec((None, KV_LORA, MLA_WIDTH), lambda b: (layer, 0, 0)),
        ],
        out_specs=pl.BlockSpec((t, MLA_WIDTH), row),
        out_shape=jax.ShapeDtypeStruct((m, MLA_WIDTH), F32),
        compiler_params=_cparams(("parallel",)),
        name="sample_mla",
    )(q, lat, kpe, cache_lat, cache_rope, wk, wv)


def _sample_band_kernel(q_ref, kn_ref, vn_ref, ck_ref, cv_ref, bc_ref, bn_ref, o_ref):
    for hd in range(H_BAND):
        a = hd * D_BAND
        q = q_ref[:, a:a + D_BAND]
        sc = _dot_nt(q, ck_ref[:, a:a + D_BAND], True) + bc_ref[hd]
        sn = _dot_nt(q, kn_ref[:, a:a + D_BAND], True) + bn_ref[hd]
        m = jnp.maximum(jnp.max(sc, axis=-1, keepdims=True), jnp.max(sn, axis=-1, keepdims=True))
        pc = jnp.exp(sc - m)
        pn = jnp.exp(sn - m)
        l = jnp.sum(pc, axis=-1, keepdims=True) + jnp.sum(pn, axis=-1, keepdims=True)
        o = _dot(pc, cv_ref[:, a:a + D_BAND], True) + _dot(pn, vn_ref[:, a:a + D_BAND], True)
        o_ref[:, a:a + D_BAND] = o / l


def _sample_band(q, kn, vn, cache_k, cache_v, bias_c, bias_n, layer, t):
    m = q.shape[0]
    batch = m // t
    r = cache_k.shape[2]
    row = lambda b: (b, 0)
    cache = lambda b: (layer, b, 0, 0)
    return pl.pallas_call(
        _sample_band_kernel,
        grid=(batch,),
        in_specs=[
            pl.BlockSpec((t, BAND_WIDTH), row),
            pl.BlockSpec((t, BAND_WIDTH), row),
            pl.BlockSpec((t, BAND_WIDTH), row),
            pl.BlockSpec((None, None, r, BAND_WIDTH), cache),
            pl.BlockSpec((None, None, r, BAND_WIDTH), cache),
            pl.BlockSpec(bias_c.shape, lambda b: (0, 0, 0)),
            pl.BlockSpec(bias_n.shape, lambda b: (0, 0, 0)),
        ],
        out_specs=pl.BlockSpec((t, BAND_WIDTH), row),
        out_shape=jax.ShapeDtypeStruct((m, BAND_WIDTH), F32),
        compiler_params=_cparams(("parallel",)),
        name="sample_band",
    )(q, kn, vn, cache_k, cache_v, bias_c, bias_n)


def _spread(w):
    z = jnp.zeros(w.shape[:-1] + (HALF_ROPE,), w.dtype)
    return jnp.concatenate([w[..., :HALF_ROPE], z, w[..., HALF_ROPE:], z], axis=-1)


def _rope_angles(pos):
    inv = 1.0 / (ROPE_THETA ** (jnp.arange(0, ROPE_DIM, 2, dtype=F32) / ROPE_DIM))
    ang = pos.astype(F32)[:, None] * inv[None, :]
    return jnp.cos(ang), jnp.sin(ang)


def _band_bias_table(rel_bias, q_pos, k_pos):
    rel = jnp.clip(q_pos[:, None] - k_pos[None, :], -REL_CLIP, REL_CLIP) + REL_CLIP
    qc = (q_pos // CHUNK)[:, None]
    kc = (k_pos // CHUNK)[None, :]
    vis = (kc <= qc) & (kc >= qc - LEFT_CHUNKS)
    return jnp.where(vis[None], rel_bias[:, rel], NEG)


def kernel(x_prompt, x_sample, cache_mla_latent, cache_mla_rope, cache_band_k, cache_band_v, w_in, g_q_lat, w_q_up, g_kv_lat, w_k_up, w_v_up, rel_bias, g_out_mla, g_out_band, w_out, ln1_g, ln1_b, w_router, b_router, w_gate, w_up, w_down, ln2_g, ln2_b):
    depth = w_in.shape[0]
    batch, seq, _ = x_prompt.shape
    dec_batch, dec_seq, _ = x_sample.shape
    past = cache_mla_latent.shape[2]
    band_r = cache_band_k.shape[2]
    alpha = float((2 * depth) ** 0.25)
    n = batch * seq
    m = dec_batch * dec_seq
    mla_scale = (NOPE_DIM + ROPE_DIM) ** -0.5
    band_scale = D_BAND ** -0.5
    a0 = Q_LORA + KV_LORA
    a1 = a0 + ROPE_DIM

    w_a = jnp.concatenate([w_in[:, :, :a0], _spread(w_in[:, :, a0:a1])], axis=-1).astype(BF16)
    w_band = jnp.concatenate([w_in[:, :, a1:a1 + BAND_WIDTH] * band_scale, w_in[:, :, a1 + BAND_WIDTH:]],
                             axis=-1).astype(BF16)
    wq4 = (w_q_up * mla_scale).reshape(depth, Q_LORA, H_MLA, NOPE_DIM + ROPE_DIM)
    wq_pad = jnp.concatenate([wq4[..., :NOPE_DIM], _spread(wq4[..., NOPE_DIM:])], axis=-1)
    wq_pad = wq_pad.reshape(depth, Q_LORA, H_MLA * QK_PAD).astype(BF16)
    wq_s = jnp.concatenate([wq4[..., :NOPE_DIM].reshape(depth, Q_LORA, -1),
                            wq4[..., NOPE_DIM:].reshape(depth, Q_LORA, -1)], axis=-1)
    wk4 = w_k_up.reshape(depth, KV_LORA, H_MLA, NOPE_DIM)
    wk_pad = jnp.concatenate([wk4, jnp.zeros_like(wk4)], axis=-1).reshape(depth, KV_LORA, H_MLA * QK_PAD)
    wk_pad = wk_pad.astype(BF16)
    wv_b = w_v_up.astype(BF16)
    w_out_b = w_out.astype(BF16)
    wg_b = w_gate.astype(BF16)
    wu_b = w_up.astype(BF16)
    wd_b = w_down.astype(BF16)
    wr_t = w_router.T
    wr_tb = wr_t.astype(BF16)
    br = b_router.reshape(N_EXPERTS, 1)
    row2 = lambda a: a.reshape(depth, 1, -1)
    gq, gkv, ga, gb = row2(g_q_lat), row2(g_kv_lat), row2(g_out_mla), row2(g_out_band)
    l1g, l1b, l2g, l2b = row2(ln1_g), row2(ln1_b), row2(ln2_g), row2(ln2_b)

    pos_p = jnp.arange(seq, dtype=jnp.int32)
    cos_p, sin_p = _rope_angles(pos_p)
    z = jnp.zeros_like(cos_p)
    cos_sp = jnp.concatenate([cos_p, z, cos_p, z], axis=-1)
    sin_sp = jnp.concatenate([-sin_p, z, sin_p, z], axis=-1)
    pos_s = past + jnp.arange(dec_seq, dtype=jnp.int32)
    cos_s, sin_s = _rope_angles(pos_s)
    cos_s = jnp.tile(jnp.concatenate([cos_s, cos_s], axis=-1), (dec_batch, H_MLA))
    sin_s = jnp.tile(jnp.concatenate([-sin_s, sin_s], axis=-1), (dec_batch, H_MLA))
    blk = jnp.arange(BAND_PAST, dtype=jnp.int32)
    kpos_blk = jnp.arange(2 * BAND_PAST, dtype=jnp.int32) - BAND_PAST
    kpos_c = past - band_r + jnp.arange(band_r, dtype=jnp.int32)

    tm_p = 256
    states = [[], [], [], [], [], [], [], []]
    xp = x_prompt.reshape(n, D_MODEL)
    xs = x_sample.reshape(m, D_MODEL)
    ck = cache_band_k.reshape(depth, dec_batch, band_r, BAND_WIDTH)
    cv = cache_band_v.reshape(depth, dec_batch, band_r, BAND_WIDTH)
    for l in range(depth):
        q, k, v, lat, kpe = _mla_proj(xp, w_a[l], gq[l], wq_pad[l], gkv[l], wk_pad[l], wv_b[l],
                                      cos_sp, sin_sp, seq, tm_p)
        qb, kb, vb, ks, vs = _band_proj(xp, w_band[l], seq, tm_p)
        oa = _mla_attn(q, k, v, seq, BAND_PAST)
        bias_p = _band_bias_table(rel_bias[l], blk, kpos_blk)
        ob = _band_attn(qb, kb, vb, bias_p, seq)
        x1, e, r, gates, cnt = _out_route(oa, ob, ga[l], gb[l], w_out_b[l], xp, l1g[l], l1b[l], wr_tb, br,
                                          alpha, tm_p, False)
        xp = _moe_ln(x1, e, r, gates, cnt, wg_b, wu_b, wd_b, l, l2g[l], l2b[l], alpha, tm_p, 256)
        for acc, s in zip(states[:4], (lat, kpe, ks, vs)):
            acc.append(s)
        sq, slat, skpe, sqb, skb, svb = _sample_proj(xs, w_in, l, gq[l], wq_s[l], gkv[l], cos_s, sin_s, 256)
        soa = _sample_mla(sq, slat, skpe, cache_mla_latent, cache_mla_rope, w_k_up, w_v_up, l, dec_seq)
        bias_c = _band_bias_table(rel_bias[l], pos_s, kpos_c)
        bias_n = _band_bias_table(rel_bias[l], pos_s, pos_s)
        sob = _sample_band(sqb, skb, svb, ck, cv, bias_c, bias_n, l, dec_seq)
        sx1, se, sr, sgates, scnt = _out_route(soa, sob, ga[l], gb[l], w_out[l], xs, l1g[l], l1b[l], wr_t, br,
                                               alpha, m, True)
        xs = _moe_ln(sx1, se, sr, sgates, scnt, wg_b, wu_b, wd_b, l, l2g[l], l2b[l], alpha, m, 128)
        for acc, s in zip(states[4:], (slat, skpe, skb, svb)):
            acc.append(s)

    st = [jnp.stack(a) for a in states]
    return (xp.reshape(batch, seq, D_MODEL), xs.reshape(dec_batch, dec_seq, D_MODEL),
            st[0].reshape(depth, batch, seq, KV_LORA), st[1].reshape(depth, batch, seq, ROPE_DIM),
            st[2].reshape(depth, batch, BAND_PAST, H_BAND, D_BAND),
            st[3].reshape(depth, batch, BAND_PAST, H_BAND, D_BAND),
            st[4].reshape(depth, dec_batch, dec_seq, KV_LORA), st[5].reshape(depth, dec_batch, dec_seq, ROPE_DIM),
            st[6].reshape(depth, dec_batch, dec_seq, H_BAND, D_BAND),
            st[7].reshape(depth, dec_batch, dec_seq, H_BAND, D_BAND))
```

```python
import functools

import jax
import jax.numpy as jnp
import numpy as np
from jax import lax
from jax.experimental import pallas as pl
from jax.experimental.pallas import tpu as pltpu

D_MODEL = 2048
CHUNK = 64
H_MLA = 8
NOPE_DIM = 128
ROPE_DIM = 64
HALF_ROPE = ROPE_DIM // 2
V_DIM = 128
Q_LORA = 512
KV_LORA = 256
ROPE_THETA = 10000.0
H_BAND = 8
D_BAND = 128
LEFT_CHUNKS = 8
BAND_PAST = LEFT_CHUNKS * CHUNK
REL_CLIP = 128
MLA_WIDTH = H_MLA * V_DIM
BAND_WIDTH = H_BAND * D_BAND
QK_PAD = 256
N_EXPERTS = 16
N_GROUPS = 4
GROUP_SIZE = 4
EXPERT_FF = 512
RMS_EPS = 1e-6
LN_EPS = 1e-5
NEG = -1e30
LOG2E = 1.4426950408889634
LANE = 128
VMEM_LIMIT = 56 * 1024 * 1024

BF16_ROWS = 16
VT_ROWS = V_DIM + BF16_ROWS

F32 = jnp.float32
BF16 = jnp.bfloat16


def _cparams(sem):
    return pltpu.CompilerParams(dimension_semantics=sem, vmem_limit_bytes=VMEM_LIMIT)


def _dot(a, b):
    return jnp.dot(a, b, preferred_element_type=F32)


def _dot_nt(a, b):
    return lax.dot_general(a, b, (((1,), (1,)), ((), ())), preferred_element_type=F32)


def _rms(x, g):
    return x * lax.rsqrt(jnp.mean(x * x, axis=-1, keepdims=True) + RMS_EPS) * g


def _layer_norm(x, g, b):
    mu = jnp.mean(x, axis=-1, keepdims=True)
    xc = x - mu
    var = jnp.mean(xc * xc, axis=-1, keepdims=True)
    return xc * lax.rsqrt(var + LN_EPS) * g + b


def _swap32(x):
    w = x.shape[-1]
    lane = lax.broadcasted_iota(jnp.int32, x.shape, x.ndim - 1)
    return jnp.where((lane % ROPE_DIM) < HALF_ROPE, pltpu.roll(x, w - HALF_ROPE, x.ndim - 1),
                     pltpu.roll(x, HALF_ROPE, x.ndim - 1))


def _mla_proj_kernel(x_ref, wa_ref, gq_ref, wq_ref, gkv_ref, wk_ref, wv_ref, c_ref, s_ref,
                     q_ref, k_ref, v_ref, lat_ref, kpe_ref):
    xb = x_ref[...].astype(BF16)
    h = _dot(xb, wa_ref[...])
    cqn = _rms(h[:, :Q_LORA], gq_ref[...])
    lat = _rms(h[:, Q_LORA:Q_LORA + KV_LORA], gkv_ref[...])
    kp = h[:, Q_LORA + KV_LORA:]
    cos = c_ref[...]
    sin = s_ref[...]
    kpr = kp * cos + pltpu.roll(kp, 2 * HALF_ROPE, 1) * sin
    lat_ref[...] = lat
    kpe_ref[...] = (kpr + pltpu.roll(kpr, LANE - HALF_ROPE, 1))[:, :ROPE_DIM]
    kprb = kpr.astype(BF16)
    latb = lat.astype(BF16)
    q = _dot(cqn.astype(BF16), wq_ref[...])
    kn = _dot(latb, wk_ref[...])
    vt = _dot_nt(wv_ref[...], latb).astype(BF16)
    ones = jnp.ones((BF16_ROWS, vt.shape[1]), BF16)
    for hd in range(H_MLA):
        v_ref[hd * VT_ROWS:hd * VT_ROWS + V_DIM, :] = vt[hd * V_DIM:(hd + 1) * V_DIM, :]
        v_ref[hd * VT_ROWS + V_DIM:(hd + 1) * VT_ROWS, :] = ones
    for hd in range(H_MLA):
        a = hd * QK_PAD
        qv = q[:, a + NOPE_DIM:a + QK_PAD]
        q_ref[:, a:a + NOPE_DIM] = q[:, a:a + NOPE_DIM].astype(BF16)
        q_ref[:, a + NOPE_DIM:a + QK_PAD] = (qv * cos + pltpu.roll(qv, 2 * HALF_ROPE, 1) * sin).astype(BF16)
        k_ref[:, a:a + NOPE_DIM] = kn[:, a:a + NOPE_DIM].astype(BF16)
        k_ref[:, a + NOPE_DIM:a + QK_PAD] = kprb


def _mla_proj(x, wa, gq, wq, gkv, wk, wv, cos, sin, seq, tm, tk):
    n = x.shape[0]
    tpb = seq // tm
    per_kv = tk // tm
    const = lambda i: (0, 0)
    row = lambda i: (i, 0)
    return pl.pallas_call(
        _mla_proj_kernel,
        grid=(n // tm,),
        in_specs=[
            pl.BlockSpec((tm, D_MODEL), row),
            pl.BlockSpec(wa.shape, const),
            pl.BlockSpec(gq.shape, const),
            pl.BlockSpec(wq.shape, const),
            pl.BlockSpec(gkv.shape, const),
            pl.BlockSpec(wk.shape, const),
            pl.BlockSpec(wv.shape, const),
            pl.BlockSpec((tm, LANE), lambda i: (i % tpb, 0)),
            pl.BlockSpec((tm, LANE), lambda i: (i % tpb, 0)),
        ],
        out_specs=[
            pl.BlockSpec((tm, H_MLA * QK_PAD), row),
            pl.BlockSpec((tm, H_MLA * QK_PAD), row),
            pl.BlockSpec((None, H_MLA * VT_ROWS, tm), lambda i: (i // per_kv, 0, i % per_kv)),
            pl.BlockSpec((tm, KV_LORA), row),
            pl.BlockSpec((tm, ROPE_DIM), row),
        ],
        out_shape=[
            jax.ShapeDtypeStruct((n, H_MLA * QK_PAD), BF16),
            jax.ShapeDtypeStruct((n, H_MLA * QK_PAD), BF16),
            jax.ShapeDtypeStruct((n // tk, H_MLA * VT_ROWS, tk), BF16),
            jax.ShapeDtypeStruct((n, KV_LORA), F32),
            jax.ShapeDtypeStruct((n, ROPE_DIM), F32),
        ],
        compiler_params=_cparams(("parallel",)),
        name="mla_proj",
    )(x, wa, gq, wq, gkv, wk, wv, cos, sin)


def _band_proj_kernel(x_ref, w_ref, wvt_ref, q_ref, k_ref, v_ref, ks_ref, vs_ref, *, tpb, ntail):
    xb = x_ref[...].astype(BF16)
    q_ref[...] = _dot(xb, w_ref[:, :BAND_WIDTH]).astype(BF16)
    k = _dot(xb, w_ref[:, BAND_WIDTH:2 * BAND_WIDTH])
    k_ref[...] = k.astype(BF16)
    vt = _dot_nt(wvt_ref[...], xb).astype(BF16)
    ones = jnp.ones((BF16_ROWS, vt.shape[1]), BF16)
    for hd in range(H_BAND):
        v_ref[hd * VT_ROWS:hd * VT_ROWS + D_BAND, :] = vt[hd * D_BAND:(hd + 1) * D_BAND, :]
        v_ref[hd * VT_ROWS + D_BAND:(hd + 1) * VT_ROWS, :] = ones

    @pl.when(pl.program_id(0) % tpb >= tpb - ntail)
    def _():
        ks_ref[...] = k
        vs_ref[...] = _dot(xb, w_ref[:, 2 * BAND_WIDTH:])


def _band_proj(x, w, wvt, seq, tm):
    n = x.shape[0]
    tpb = seq // tm
    ntail = BAND_PAST // tm
    batch = n // seq
    const = lambda i: (0, 0)
    row = lambda i: (i, 0)
    tail = lambda i: (i // tpb, jnp.maximum(i % tpb - (tpb - ntail), 0), 0)
    return pl.pallas_call(
        functools.partial(_band_proj_kernel, tpb=tpb, ntail=ntail),
        grid=(n // tm,),
        in_specs=[pl.BlockSpec((tm, D_MODEL), row), pl.BlockSpec(w.shape, const), pl.BlockSpec(wvt.shape, const)],
        out_specs=[
            pl.BlockSpec((tm, BAND_WIDTH), row),
            pl.BlockSpec((tm, BAND_WIDTH), row),
            pl.BlockSpec((None, H_BAND * VT_ROWS, tm), lambda i: (i // ntail, 0, i % ntail)),
            pl.BlockSpec((None, tm, BAND_WIDTH), tail),
            pl.BlockSpec((None, tm, BAND_WIDTH), tail),
        ],
        out_shape=[
            jax.ShapeDtypeStruct((n, BAND_WIDTH), BF16),
            jax.ShapeDtypeStruct((n, BAND_WIDTH), BF16),
            jax.ShapeDtypeStruct((n // BAND_PAST, H_BAND * VT_ROWS, BAND_PAST), BF16),
            jax.ShapeDtypeStruct((batch, BAND_PAST, BAND_WIDTH), F32),
            jax.ShapeDtypeStruct((batch, BAND_PAST, BAND_WIDTH), F32),
        ],
        compiler_params=_cparams(("arbitrary",)),
        name="band_proj",
    )(x, w, wvt)


def _attn_out(acc):
    return (acc[:V_DIM, :] / acc[V_DIM:V_DIM + 1, :]).T


def _mla_attn_kernel(q_ref, k_ref, vt_ref, o_ref, *, tq):
    qi = pl.program_id(2)
    q = q_ref[...]

    def scores(j):
        off = pl.multiple_of(j * tq, tq)
        return _dot_nt(k_ref[pl.ds(off, tq), :], q)

    def fold(j, st, m, acc):
        m_new = jnp.maximum(m, jnp.max(st, axis=0, keepdims=True))
        p = jnp.exp2(st - m_new)
        return m_new, jnp.exp2(m - m_new) * acc + _dot(vt_ref[j], p.astype(BF16))

    def body(j, carry):
        m, acc, st = carry
        st_next = scores(j + 1)
        m, acc = fold(j, st, m, acc)
        return m, acc, st_next

    init = (jnp.full((1, tq), NEG, F32), jnp.zeros((VT_ROWS, tq), F32), scores(0))
    m, acc, st = lax.fori_loop(0, qi, body, init)
    kc = lax.broadcasted_iota(jnp.int32, st.shape, 0) // CHUNK
    qc = lax.broadcasted_iota(jnp.int32, st.shape, 1) // CHUNK
    _, acc = fold(qi, jnp.where(kc <= qc, st, NEG), m, acc)
    o_ref[...] = _attn_out(acc).astype(o_ref.dtype)


def _mla_attn(q, k, vt, seq, tq):
    n = q.shape[0]
    batch = n // seq
    nq = seq // tq
    return pl.pallas_call(
        functools.partial(_mla_attn_kernel, tq=tq),
        grid=(batch, H_MLA, nq),
        in_specs=[
            pl.BlockSpec((tq, QK_PAD), lambda b, h, i: (b * nq + i, h)),
            pl.BlockSpec((seq, QK_PAD), lambda b, h, i: (b, h)),
            pl.BlockSpec((nq, VT_ROWS, tq), lambda b, h, i: (b, h, 0)),
        ],
        out_specs=pl.BlockSpec((tq, V_DIM), lambda b, h, i: (b * nq + i, h)),
        out_shape=jax.ShapeDtypeStruct((n, MLA_WIDTH), BF16),
        compiler_params=_cparams(("parallel", "parallel", "arbitrary")),
        name="mla_attn",
    )(q, k, vt)


def _band_attn_kernel(q_ref, k_ref, vt_ref, b_ref, o_ref, *, nq):
    tq = BAND_PAST
    for i in range(nq):
        q = q_ref[i * tq:(i + 1) * tq, :]
        sc = _dot_nt(k_ref[i * tq:(i + 1) * tq, :], q) + b_ref[tq:, :]
        m = jnp.max(sc, axis=0, keepdims=True)
        if i > 0:
            sp = _dot_nt(k_ref[(i - 1) * tq:i * tq, :], q) + b_ref[:tq, :]
            m = jnp.maximum(m, jnp.max(sp, axis=0, keepdims=True))
            acc = _dot(vt_ref[i - 1], jnp.exp2(sp - m).astype(BF16))
            acc = acc + _dot(vt_ref[i], jnp.exp2(sc - m).astype(BF16))
        else:
            acc = _dot(vt_ref[i], jnp.exp2(sc - m).astype(BF16))
        o_ref[i * tq:(i + 1) * tq, :] = _attn_out(acc).astype(o_ref.dtype)


def _band_attn(q, k, vt, bias_t, seq):
    n = q.shape[0]
    batch = n // seq
    nq = seq // BAND_PAST
    blk = lambda b, h: (b, h)
    return pl.pallas_call(
        functools.partial(_band_attn_kernel, nq=nq),
        grid=(batch, H_BAND),
        in_specs=[
            pl.BlockSpec((seq, D_BAND), blk),
            pl.BlockSpec((seq, D_BAND), blk),
            pl.BlockSpec((nq, VT_ROWS, BAND_PAST), lambda b, h: (b, h, 0)),
            pl.BlockSpec((None, 2 * BAND_PAST, BAND_PAST), lambda b, h: (h, 0, 0)),
        ],
        out_specs=pl.BlockSpec((seq, D_BAND), blk),
        out_shape=jax.ShapeDtypeStruct((n, BAND_WIDTH), BF16),
        compiler_params=_cparams(("parallel", "parallel")),
        name="band_attn",
    )(q, k, vt, bias_t)


def _top2_of4(v, sv):
    best, bi, bs = v[0], jnp.zeros(v[0].shape, jnp.int32), sv[0]
    for i in range(1, GROUP_SIZE):
        c = v[i] > best
        best = jnp.where(c, v[i], best)
        bi = jnp.where(c, i, bi)
        bs = jnp.where(c, sv[i], bs)
    sec = jnp.full(v[0].shape, -jnp.inf, F32)
    si = jnp.zeros(v[0].shape, jnp.int32)
    ss = jnp.zeros(v[0].shape, F32)
    for i in range(GROUP_SIZE):
        c = (bi != i) & (v[i] > sec)
        sec = jnp.where(c, v[i], sec)
        si = jnp.where(c, i, si)
        ss = jnp.where(c, sv[i], ss)
    return best + sec, bi, si, bs, ss


def _route(sel, s):
    groups = []
    for g in range(N_GROUPS):
        rows = range(g * GROUP_SIZE, (g + 1) * GROUP_SIZE)
        groups.append(_top2_of4([sel[i:i + 1, :] for i in rows], [s[i:i + 1, :] for i in rows]))
    score, bi, si, bs, ss = groups[0]
    gi = jnp.zeros(score.shape, jnp.int32)
    for g in range(1, N_GROUPS):
        c = groups[g][0] > score
        score = jnp.where(c, groups[g][0], score)
        gi = jnp.where(c, g, gi)
        bi = jnp.where(c, groups[g][1], bi)
        si = jnp.where(c, groups[g][2], si)
        bs = jnp.where(c, groups[g][3], bs)
        ss = jnp.where(c, groups[g][4], ss)
    den = bs + ss
    return gi * GROUP_SIZE + bi, gi * GROUP_SIZE + si, bs / den, ss / den


def _out_route_kernel(oa_ref, ob_ref, ga_ref, gb_ref, wo_ref, x_ref, lg_ref, lb_ref, wr_ref, br_ref, u_ref,
                      x1_ref, e_ref, r_ref, g_ref, cnt_ref, base_ref, *, alpha):
    @pl.when(pl.program_id(0) == 0)
    def _():
        base_ref[...] = jnp.zeros_like(base_ref)

    na = _rms(oa_ref[...].astype(F32), ga_ref[...]).astype(BF16)
    nb = _rms(ob_ref[...].astype(F32), gb_ref[...]).astype(BF16)
    mix = _dot(na, wo_ref[:MLA_WIDTH, :]) + _dot(nb, wo_ref[MLA_WIDTH:, :])
    x1 = _layer_norm(alpha * x_ref[...] + mix, lg_ref[...], lb_ref[...])
    x1_ref[...] = x1
    logits = _dot_nt(wr_ref[...], x1.astype(BF16))
    s = 1.0 / (1.0 + jnp.exp(-logits))
    e0, e1, g0, g1 = _route(s + br_ref[...], s)
    e_ref[0:1, :] = e0
    e_ref[1:2, :] = e1
    g_ref[0:1, :] = g0
    g_ref[1:2, :] = g1
    eid = lax.broadcasted_iota(jnp.int32, logits.shape, 0)
    oh0 = eid == e0
    oh1 = eid == e1
    both = jnp.where(oh0 | oh1, 1.0, 0.0)
    before = _dot(both.astype(BF16), u_ref[...]) + base_ref[...]
    r_ref[0:1, :] = jnp.sum(jnp.where(oh0, before, 0.0), axis=0, keepdims=True).astype(jnp.int32)
    r_ref[1:2, :] = jnp.sum(jnp.where(oh1, before, 0.0), axis=0, keepdims=True).astype(jnp.int32)
    base = base_ref[...] + jnp.sum(both, axis=1, keepdims=True)
    base_ref[...] = base
    cnt_ref[...] = base


def _out_route(oa, ob, ga, gb, wo, x, lg, lb, wr, br, alpha, tm):
    n = x.shape[0]
    u = (lax.broadcasted_iota(jnp.int32, (tm, tm), 0) < lax.broadcasted_iota(jnp.int32, (tm, tm), 1)).astype(BF16)
    const = lambda i: (0, 0)
    row = lambda i: (i, 0)
    col = lambda i: (0, i)
    return pl.pallas_call(
        functools.partial(_out_route_kernel, alpha=alpha),
        grid=(n // tm,),
        in_specs=[
            pl.BlockSpec((tm, MLA_WIDTH), row),
            pl.BlockSpec((tm, BAND_WIDTH), row),
            pl.BlockSpec(ga.shape, const),
            pl.BlockSpec(gb.shape, const),
            pl.BlockSpec(wo.shape, const),
            pl.BlockSpec((tm, D_MODEL), row),
            pl.BlockSpec(lg.shape, const),
            pl.BlockSpec(lb.shape, const),
            pl.BlockSpec(wr.shape, const),
            pl.BlockSpec(br.shape, const),
            pl.BlockSpec((tm, tm), const),
        ],
        out_specs=[
            pl.BlockSpec((tm, D_MODEL), row),
            pl.BlockSpec((2, tm), col),
            pl.BlockSpec((2, tm), col),
            pl.BlockSpec((2, tm), col),
            pl.BlockSpec((N_EXPERTS, 1), const),
        ],
        out_shape=[
            jax.ShapeDtypeStruct((n, D_MODEL), F32),
            jax.ShapeDtypeStruct((2, n), jnp.int32),
            jax.ShapeDtypeStruct((2, n), jnp.int32),
            jax.ShapeDtypeStruct((2, n), F32),
            jax.ShapeDtypeStruct((N_EXPERTS, 1), F32),
        ],
        scratch_shapes=[pltpu.VMEM((N_EXPERTS, 1), F32)],
        compiler_params=_cparams(("arbitrary",)),
        name="out_route",
    )(oa, ob, ga, gb, wo, x, lg, lb, wr, br, u)


def _slots_kernel(cnt_ref, e_ref, r_ref, d_ref, blk_ref, ex_ref, lo_ref, hi_ref, *, tb, nblocks):
    shift = tb.bit_length() - 1
    e = e_ref[...]
    dest = r_ref[...]
    item = lax.broadcasted_iota(jnp.int32, blk_ref.shape, 1)
    blk = jnp.zeros(blk_ref.shape, jnp.int32)
    exv = jnp.zeros(blk_ref.shape, jnp.int32)
    lo = jnp.zeros(blk_ref.shape, jnp.int32)
    hi = jnp.zeros(blk_ref.shape, jnp.int32)
    start = jnp.int32(0)
    first = jnp.int32(0)
    last_e = jnp.int32(0)
    for ex in range(N_EXPERTS):
        cnt = cnt_ref[ex]
        end = start + cnt
        sb = lax.shift_right_logical(start, shift)
        nitems = jnp.where(cnt > 0, lax.shift_right_logical(end - 1, shift) - sb + 1, 0)
        dest = dest + jnp.where(e == ex, start, 0)
        mine = (item >= first) & (item < first + nitems)
        b = sb + item - first
        blk = jnp.where(mine, b, blk)
        exv = jnp.where(mine, ex, exv)
        lo = jnp.where(mine, jnp.maximum(start, b * tb) - b * tb, lo)
        hi = jnp.where(mine, jnp.minimum(end, (b + 1) * tb) - b * tb, hi)
        last_e = jnp.where(cnt > 0, ex, last_e)
        start = end
        first = first + nitems
    unused = item >= first
    d_ref[...] = dest
    blk_ref[...] = jnp.where(unused, nblocks - 1, blk)
    ex_ref[...] = jnp.where(unused, last_e, exv)
    lo_ref[...] = lo
    hi_ref[...] = hi


def _slots(cnt, e, r, tb, nwp):
    n = e.shape[1]
    full = lambda shape: pl.BlockSpec(shape, lambda: (0, 0))
    return pl.pallas_call(
        functools.partial(_slots_kernel, tb=tb, nblocks=2 * n // tb),
        in_specs=[pl.BlockSpec(memory_space=pltpu.SMEM), full((2, n)), full((2, n))],
        out_specs=[full((2, n))] + [full((1, nwp))] * 4,
        out_shape=[jax.ShapeDtypeStruct((2, n), jnp.int32)] + [jax.ShapeDtypeStruct((1, nwp), jnp.int32)] * 4,
        name="moe_slots",
    )(cnt, e, r)


def _dispatch_kernel(d_ref, x_ref, xs_ref, sem, *, tm):
    def issue(r, c):
        for k in range(2):
            pltpu.make_async_copy(x_ref.at[pl.ds(r, 1)], xs_ref.at[pl.ds(d_ref[0, 0, k * tm + r], 1)], sem).start()
        return c

    lax.fori_loop(0, tm, issue, 0)

    def drain(r, c):
        for k in range(2):
            pltpu.make_async_copy(x_ref.at[pl.ds(0, 1)], xs_ref.at[pl.ds(0, 1)], sem).wait()
        return c

    lax.fori_loop(0, tm, drain, 0)


def _dispatch(dest3, x1, nslots, tm):
    n = x1.shape[0]
    return pl.pallas_call(
        functools.partial(_dispatch_kernel, tm=tm),
        grid=(n // tm,),
        in_specs=[
            pl.BlockSpec((1, 1, 2 * tm), lambda i: (i, 0, 0), memory_space=pltpu.SMEM),
            pl.BlockSpec((tm, D_MODEL), lambda i: (i, 0)),
        ],
        out_specs=pl.BlockSpec(memory_space=pl.ANY),
        out_shape=jax.ShapeDtypeStruct((nslots, D_MODEL), F32),
        scratch_shapes=[pltpu.SemaphoreType.DMA(())],
        compiler_params=_cparams(("arbitrary",)),
        name="moe_dispatch",
    )(dest3, x1)


def _experts_kernel(blk_ref, ex_ref, lo_ref, hi_ref, xs_ref, wg_ref, wu_ref, wd_ref, y_ref):
    i = pl.program_id(0)
    lo = lo_ref[i]
    hi = hi_ref[i]
    opens_block = (i == 0) | (blk_ref[i] != blk_ref[jnp.maximum(i - 1, 0)])

    @pl.when(lo < hi)
    def _():
        xb = xs_ref[...].astype(BF16)
        g = _dot(xb, wg_ref[...])
        u = _dot(xb, wu_ref[...])
        hid = (g / (1.0 + jnp.exp(-g))) * u
        y = _dot(hid.astype(BF16), wd_ref[...])
        row = lax.broadcasted_iota(jnp.int32, (y.shape[0], 1), 0)
        y = jnp.where((row >= lo) & (row < hi), y, 0.0)

        @pl.when(opens_block)
        def _():
            y_ref[...] = y

        @pl.when(jnp.logical_not(opens_block))
        def _():
            y_ref[...] += y


def _experts(blk, ex, lo, hi, xs, wg, wu, wd, layer, tb, nwork):
    nslots = xs.shape[0]
    rows = lambda i, blk, ex, lo, hi: (blk[i], 0)
    wmap = lambda i, blk, ex, lo, hi: (layer, ex[i], 0, 0)
    return pl.pallas_call(
        _experts_kernel,
        grid_spec=pltpu.PrefetchScalarGridSpec(
            num_scalar_prefetch=4,
            grid=(nwork,),
            in_specs=[
                pl.BlockSpec((tb, D_MODEL), rows),
                pl.BlockSpec((None, None, D_MODEL, EXPERT_FF), wmap),
                pl.BlockSpec((None, None, D_MODEL, EXPERT_FF), wmap),
                pl.BlockSpec((None, None, EXPERT_FF, D_MODEL), wmap),
            ],
            out_specs=pl.BlockSpec((tb, D_MODEL), rows),
        ),
        out_shape=jax.ShapeDtypeStruct((nslots, D_MODEL), F32),
        compiler_params=_cparams(("arbitrary",)),
        name="moe_experts",
    )(blk, ex, lo, hi, xs, wg, wu, wd)


def _combine_kernel(d_ref, x_ref, gt_ref, lg_ref, lb_ref, y_ref, o_ref, buf, sem, *, tm, alpha):
    def issue(r, c):
        for k in range(2):
            pltpu.make_async_copy(y_ref.at[pl.ds(d_ref[0, 0, k * tm + r], 1)], buf.at[k, pl.ds(r, 1)], sem).start()
        return c

    lax.fori_loop(0, tm, issue, 0)

    def drain(r, c):
        for k in range(2):
            pltpu.make_async_copy(y_ref.at[pl.ds(0, 1)], buf.at[0, pl.ds(0, 1)], sem).wait()
        return c

    lax.fori_loop(0, tm, drain, 0)
    ffn = buf[0] * gt_ref[:, 0:1] + buf[1] * gt_ref[:, 1:2]
    o_ref[...] = _layer_norm(alpha * x_ref[...] + ffn, lg_ref[...], lb_ref[...])


def _combine(dest3, x1, gates_t, lg, lb, yb, alpha, tm):
    n = x1.shape[0]
    const = lambda i: (0, 0)
    row = lambda i: (i, 0)
    return pl.pallas_call(
        functools.partial(_combine_kernel, tm=tm, alpha=alpha),
        grid=(n // tm,),
        in_specs=[
            pl.BlockSpec((1, 1, 2 * tm), lambda i: (i, 0, 0), memory_space=pltpu.SMEM),
            pl.BlockSpec((tm, D_MODEL), row),
            pl.BlockSpec((tm, 2), row),
            pl.BlockSpec(lg.shape, const),
            pl.BlockSpec(lb.shape, const),
            pl.BlockSpec(memory_space=pl.ANY),
        ],
        out_specs=pl.BlockSpec((tm, D_MODEL), row),
        out_shape=jax.ShapeDtypeStruct((n, D_MODEL), F32),
        scratch_shapes=[pltpu.VMEM((2, tm, D_MODEL), F32), pltpu.SemaphoreType.DMA(())],
        compiler_params=_cparams(("arbitrary",)),
        name="moe_combine",
    )(dest3, x1, gates_t, lg, lb, yb)


def _moe_ln(x1, e, r, gates, cnt, wg, wu, wd, layer, lg, lb, alpha, tm, tb):
    n = x1.shape[0]
    nwork = (2 * n) // tb + N_EXPERTS - 1
    nwp = -(-nwork // LANE) * LANE
    dest, blk, ex, lo, hi = _slots(cnt.astype(jnp.int32).reshape(N_EXPERTS), e, r, tb, nwp)
    dest3 = dest.reshape(2, n // tm, tm).transpose(1, 0, 2).reshape(n // tm, 1, 2 * tm)
    xs = _dispatch(dest3, x1, 2 * n, tm)
    yb = _experts(blk.reshape(nwp), ex.reshape(nwp), lo.reshape(nwp), hi.reshape(nwp), xs, wg, wu, wd,
                  layer, tb, nwork)
    return _combine(dest3, x1, gates.T, lg, lb, yb, alpha, tm)


def _sample_proj_kernel(x_ref, w_ref, gq_ref, wq_ref, gkv_ref, c_ref, s_ref,
                        q_ref, lat_ref, kpe_ref, qb_ref, kb_ref, vb_ref, acc_ref):
    k = pl.program_id(0)

    @pl.when(k == 0)
    def _():
        acc_ref[...] = jnp.zeros_like(acc_ref)

    acc_ref[...] += _dot(x_ref[...].astype(BF16), w_ref[...].astype(BF16))

    @pl.when(k == pl.num_programs(0) - 1)
    def _():
        cos = c_ref[...]
        sin = s_ref[...]
        cqn = _rms(acc_ref[:, :Q_LORA], gq_ref[...])
        q = _dot(cqn.astype(BF16), wq_ref[...])
        q_ref[:, :H_MLA * NOPE_DIM] = q[:, :H_MLA * NOPE_DIM]
        qr = q[:, H_MLA * NOPE_DIM:]
        q_ref[:, H_MLA * NOPE_DIM:] = qr * cos + _swap32(qr) * sin
        lat_ref[...] = _rms(acc_ref[:, Q_LORA:Q_LORA + KV_LORA], gkv_ref[...])
        a = Q_LORA + KV_LORA
        kv = acc_ref[:, a:a + LANE]
        kpe_ref[...] = (kv * cos[:, :LANE] + _swap32(kv) * sin[:, :LANE])[:, :ROPE_DIM]
        a += ROPE_DIM
        qb_ref[...] = acc_ref[:, a:a + BAND_WIDTH]
        kb_ref[...] = acc_ref[:, a + BAND_WIDTH:a + 2 * BAND_WIDTH]
        vb_ref[...] = acc_ref[:, a + 2 * BAND_WIDTH:a + 3 * BAND_WIDTH]


def _sample_proj(x, w_in, layer, gq, wq, gkv, cos, sin, tk):
    m = x.shape[0]
    width = w_in.shape[2]
    const = lambda k: (0, 0)
    shapes = [(m, wq.shape[1]), (m, KV_LORA), (m, ROPE_DIM), (m, BAND_WIDTH), (m, BAND_WIDTH), (m, BAND_WIDTH)]
    return pl.pallas_call(
        _sample_proj_kernel,
        grid=(D_MODEL // tk,),
        in_specs=[
            pl.BlockSpec((m, tk), lambda k: (0, k)),
            pl.BlockSpec((None, tk, width), lambda k: (layer, k, 0)),
            pl.BlockSpec(gq.shape, const),
            pl.BlockSpec(wq.shape, const),
            pl.BlockSpec(gkv.shape, const),
            pl.BlockSpec(cos.shape, const),
            pl.BlockSpec(sin.shape, const),
        ],
        out_specs=[pl.BlockSpec(s, const) for s in shapes],
        out_shape=[jax.ShapeDtypeStruct(s, F32) for s in shapes],
        scratch_shapes=[pltpu.VMEM((m, width), F32)],
        compiler_params=_cparams(("arbitrary",)),
        name="sample_proj",
    )(x, w_in, gq, wq, gkv, cos, sin)


def _softmax2(sc, sn):
    m = jnp.maximum(jnp.max(sc, axis=-1, keepdims=True), jnp.max(sn, axis=-1, keepdims=True))
    ec = jnp.exp(sc - m)
    en = jnp.exp(sn - m)
    l = jnp.sum(ec, axis=-1, keepdims=True) + jnp.sum(en, axis=-1, keepdims=True)
    return (ec / l).astype(BF16), (en / l).astype(BF16)


def _sample_mla_kernel(q_ref, lat_ref, kpe_ref, cl_ref, cr_ref, wk_ref, wv_ref, o_ref, *, scale):
    lat_c = cl_ref[...].astype(BF16)
    lat_n = lat_ref[...].astype(BF16)
    kpe_c = cr_ref[...].astype(BF16)
    kpe_n = kpe_ref[...].astype(BF16)
    kc = _dot(lat_c, wk_ref[...]).astype(BF16)
    vc = _dot(lat_c, wv_ref[...]).astype(BF16)
    kn = _dot(lat_n, wk_ref[...]).astype(BF16)
    vn = _dot(lat_n, wv_ref[...]).astype(BF16)
    for hd in range(H_MLA):
        a = hd * NOPE_DIM
        b = H_MLA * NOPE_DIM + hd * ROPE_DIM
        qn = q_ref[:, a:a + NOPE_DIM].astype(BF16)
        qr = q_ref[:, b:b + ROPE_DIM].astype(BF16)
        sc = (_dot_nt(qn, kc[:, a:a + NOPE_DIM]) + _dot_nt(qr, kpe_c)) * scale
        sn = (_dot_nt(qn, kn[:, a:a + NOPE_DIM]) + _dot_nt(qr, kpe_n)) * scale
        pc, pn = _softmax2(sc, sn)
        o_ref[:, a:a + V_DIM] = _dot(pc, vc[:, a:a + V_DIM]) + _dot(pn, vn[:, a:a + V_DIM])


def _sample_mla(q, lat, kpe, cache_lat, cache_rope, wk, wv, layer, t):
    m = q.shape[0]
    batch = m // t
    past = cache_lat.shape[2]
    row = lambda b: (b, 0)
    const = lambda b: (0, 0)
    return pl.pallas_call(
        functools.partial(_sample_mla_kernel, scale=(NOPE_DIM + ROPE_DIM) ** -0.5),
        grid=(batch,),
        in_specs=[
            pl.BlockSpec((t, q.shape[1]), row),
            pl.BlockSpec((t, KV_LORA), row),
            pl.BlockSpec((t, ROPE_DIM), row),
            pl.BlockSpec((None, None, past, KV_LORA), lambda b: (layer, b, 0, 0)),
            pl.BlockSpec((None, None, past, ROPE_DIM), lambda b: (layer, b, 0, 0)),
            pl.BlockSpec((None, KV_LORA, MLA_WIDTH), lambda b: (layer, 0, 0)),
            pl.BlockSpec((None, KV_LORA, MLA_WIDTH), lambda b: (layer, 0, 0)),
        ],
        out_specs=pl.BlockSpec((t, MLA_WIDTH), row),
        out_shape=jax.ShapeDtypeStruct((m, MLA_WIDTH), F32),
        compiler_params=_cparams(("parallel",)),
        name="sample_mla",
    )(q, lat, kpe, cache_lat, cache_rope, wk, wv)


def _sample_band_kernel(q_ref, kn_ref, vn_ref, ck_ref, cv_ref, bc_ref, bn_ref, o_ref, *, scale):
    for hd in range(H_BAND):
        a = hd * D_BAND
        q = q_ref[:, a:a + D_BAND].astype(BF16)
        sc = _dot_nt(q, ck_ref[:, a:a + D_BAND].astype(BF16)) * scale + bc_ref[hd]
        sn = _dot_nt(q, kn_ref[:, a:a + D_BAND].astype(BF16)) * scale + bn_ref[hd]
        pc, pn = _softmax2(sc, sn)
        o_ref[:, a:a + D_BAND] = (_dot(pc, cv_ref[:, a:a + D_BAND].astype(BF16))
                                  + _dot(pn, vn_ref[:, a:a + D_BAND].astype(BF16)))


def _sample_band(q, kn, vn, cache_k, cache_v, bias_c, bias_n, layer, t):
    m = q.shape[0]
    batch = m // t
    r = cache_k.shape[2]
    row = lambda b: (b, 0)
    cache = lambda b: (layer, b, 0, 0)
    return pl.pallas_call(
        functools.partial(_sample_band_kernel, scale=D_BAND ** -0.5),
        grid=(batch,),
        in_specs=[
            pl.BlockSpec((t, BAND_WIDTH), row),
            pl.BlockSpec((t, BAND_WIDTH), row),
            pl.BlockSpec((t, BAND_WIDTH), row),
            pl.BlockSpec((None, None, r, BAND_WIDTH), cache),
            pl.BlockSpec((None, None, r, BAND_WIDTH), cache),
            pl.BlockSpec(bias_c.shape, lambda b: (0, 0, 0)),
            pl.BlockSpec(bias_n.shape, lambda b: (0, 0, 0)),
        ],
        out_specs=pl.BlockSpec((t, BAND_WIDTH), row),
        out_shape=jax.ShapeDtypeStruct((m, BAND_WIDTH), F32),
        compiler_params=_cparams(("parallel",)),
        name="sample_band",
    )(q, kn, vn, cache_k, cache_v, bias_c, bias_n)


def _spread(w):
    z = jnp.zeros(w.shape[:-1] + (HALF_ROPE,), w.dtype)
    return jnp.concatenate([w[..., :HALF_ROPE], z, w[..., HALF_ROPE:], z], axis=-1)


def _rope_angles(pos):
    inv = 1.0 / (ROPE_THETA ** (jnp.arange(0, ROPE_DIM, 2, dtype=F32) / ROPE_DIM))
    ang = pos.astype(F32)[:, None] * inv[None, :]
    return jnp.cos(ang), jnp.sin(ang)


def _band_bias_table(rel_bias, q0, nq, k0, nk, keys_first):
    nr, nc = (nk, nq) if keys_first else (nq, nk)
    period = nr + nc
    delta = np.arange(period)
    delta = np.where(delta < nc, delta, delta - period)
    qk = (q0 - k0) + (delta if keys_first else -delta)
    line = rel_bias[:, np.clip(qk, -REL_CLIP, REL_CLIP) + REL_CLIP]
    table = jnp.tile(line, (1, nr))[:, :nr * (period - 1)].reshape(-1, nr, period - 1)[:, :, :nc]
    qc = (q0 + np.arange(nq)) // CHUNK
    kc = (k0 + np.arange(nk)) // CHUNK
    vis = (kc[None, :] <= qc[:, None]) & (kc[None, :] >= qc[:, None] - LEFT_CHUNKS)
    return jnp.where(jnp.asarray(vis.T if keys_first else vis)[None], table, NEG)


def kernel(x_prompt, x_sample, cache_mla_latent, cache_mla_rope, cache_band_k, cache_band_v, w_in, g_q_lat, w_q_up, g_kv_lat, w_k_up, w_v_up, rel_bias, g_out_mla, g_out_band, w_out, ln1_g, ln1_b, w_router, b_router, w_gate, w_up, w_down, ln2_g, ln2_b):
    depth = w_in.shape[0]
    batch, seq, _ = x_prompt.shape
    dec_batch, dec_seq, _ = x_sample.shape
    past = cache_mla_latent.shape[2]
    band_r = cache_band_k.shape[2]
    alpha = float((2 * depth) ** 0.25)
    n = batch * seq
    m = dec_batch * dec_seq
    mla_scale = (NOPE_DIM + ROPE_DIM) ** -0.5 * LOG2E
    band_scale = D_BAND ** -0.5 * LOG2E
    a0 = Q_LORA + KV_LORA
    a1 = a0 + ROPE_DIM

    w_a = jnp.concatenate([w_in[:, :, :a0], _spread(w_in[:, :, a0:a1])], axis=-1).astype(BF16)
    w_band = jnp.concatenate([w_in[:, :, a1:a1 + BAND_WIDTH] * band_scale, w_in[:, :, a1 + BAND_WIDTH:]],
                             axis=-1).astype(BF16)
    w_band_vt = w_in[:, :, a1 + 2 * BAND_WIDTH:].transpose(0, 2, 1).astype(BF16)
    wq4 = w_q_up.reshape(depth, Q_LORA, H_MLA, NOPE_DIM + ROPE_DIM)
    wq_pad = jnp.concatenate([wq4[..., :NOPE_DIM], _spread(wq4[..., NOPE_DIM:])], axis=-1) * mla_scale
    wq_pad = wq_pad.reshape(depth, Q_LORA, H_MLA * QK_PAD).astype(BF16)
    wq_s = jnp.concatenate([wq4[..., :NOPE_DIM].reshape(depth, Q_LORA, -1),
                            wq4[..., NOPE_DIM:].reshape(depth, Q_LORA, -1)], axis=-1).astype(BF16)
    wk4 = w_k_up.reshape(depth, KV_LORA, H_MLA, NOPE_DIM)
    wk_pad = jnp.concatenate([wk4, jnp.zeros_like(wk4)], axis=-1).reshape(depth, KV_LORA, H_MLA * QK_PAD)
    wk_pad = wk_pad.astype(BF16)
    wk_b = w_k_up.astype(BF16)
    wv_b = w_v_up.astype(BF16)
    wv_t = w_v_up.transpose(0, 2, 1).astype(BF16)
    w_out_b = w_out.astype(BF16)
    wg_b = w_gate.astype(BF16)
    wu_b = w_up.astype(BF16)
    wd_b = w_down.astype(BF16)
    wr_tb = w_router.T.astype(BF16)
    br = b_router.reshape(N_EXPERTS, 1)
    row2 = lambda a: a.reshape(depth, 1, -1)
    gq, gkv, ga, gb = row2(g_q_lat), row2(g_kv_lat), row2(g_out_mla), row2(g_out_band)
    l1g, l1b, l2g, l2b = row2(ln1_g), row2(ln1_b), row2(ln2_g), row2(ln2_b)

    pos_p = jnp.arange(seq, dtype=jnp.int32)
    cos_p, sin_p = _rope_angles(pos_p)
    z = jnp.zeros_like(cos_p)
    cos_sp = jnp.concatenate([cos_p, z, cos_p, z], axis=-1)
    sin_sp = jnp.concatenate([-sin_p, z, sin_p, z], axis=-1)
    pos_s = past + jnp.arange(dec_seq, dtype=jnp.int32)
    cos_s, sin_s = _rope_angles(pos_s)
    cos_s = jnp.tile(jnp.concatenate([cos_s, cos_s], axis=-1), (dec_batch, H_MLA))
    sin_s = jnp.tile(jnp.concatenate([-sin_s, sin_s], axis=-1), (dec_batch, H_MLA))

    tm_p = 256
    states = [[], [], [], [], [], [], [], []]
    xp = x_prompt.reshape(n, D_MODEL)
    xs = x_sample.reshape(m, D_MODEL)
    ck = cache_band_k.reshape(depth, dec_batch, band_r, BAND_WIDTH)
    cv = cache_band_v.reshape(depth, dec_batch, band_r, BAND_WIDTH)
    for l in range(depth):
        q, k, vt, lat, kpe = _mla_proj(xp, w_a[l], gq[l], wq_pad[l], gkv[l], wk_pad[l], wv_t[l],
                                       cos_sp, sin_sp, seq, tm_p, BAND_PAST)
        qb, kb, vbt, ks, vs = _band_proj(xp, w_band[l], w_band_vt[l], seq, tm_p)
        oa = _mla_attn(q, k, vt, seq, BAND_PAST)
        bias_p = _band_bias_table(rel_bias[l] * LOG2E, 0, BAND_PAST, -BAND_PAST, 2 * BAND_PAST, True)
        ob = _band_attn(qb, kb, vbt, bias_p, seq)
        x1, e, r, gates, cnt = _out_route(oa, ob, ga[l], gb[l], w_out_b[l], xp, l1g[l], l1b[l], wr_tb, br,
                                          alpha, tm_p)
        xp = _moe_ln(x1, e, r, gates, cnt, wg_b, wu_b, wd_b, l, l2g[l], l2b[l], alpha, tm_p, 256)
        for acc, s in zip(states[:4], (lat, kpe, ks, vs)):
            acc.append(s)
        sq, slat, skpe, sqb, skb, svb = _sample_proj(xs, w_in, l, gq[l], wq_s[l], gkv[l], cos_s, sin_s, 256)
        soa = _sample_mla(sq, slat, skpe, cache_mla_latent, cache_mla_rope, wk_b, wv_b, l, dec_seq)
        bias_c = _band_bias_table(rel_bias[l], past, dec_seq, past - band_r, band_r, False)
        bias_n = _band_bias_table(rel_bias[l], past, dec_seq, past, dec_seq, False)
        sob = _sample_band(sqb, skb, svb, ck, cv, bias_c, bias_n, l, dec_seq)
        sx1, se, sr, sgates, scnt = _out_route(soa, sob, ga[l], gb[l], w_out_b[l], xs, l1g[l], l1b[l], wr_tb, br,
                                               alpha, m)
        xs = _moe_ln(sx1, se, sr, sgates, scnt, wg_b, wu_b, wd_b, l, l2g[l], l2b[l], alpha, m, 128)
        for acc, s in zip(states[4:], (slat, skpe, skb, svb)):
            acc.append(s)

    st = [jnp.stack(a) for a in states]
    return (xp.reshape(batch, seq, D_MODEL), xs.reshape(dec_batch, dec_seq, D_MODEL),
            st[0].reshape(depth, batch, seq, KV_LORA), st[1].reshape(depth, batch, seq, ROPE_DIM),
            st[2].reshape(depth, batch, BAND_PAST, H_BAND, D_BAND),
            st[3].reshape(depth, batch, BAND_PAST, H_BAND, D_BAND),
            st[4].reshape(depth, dec_batch, dec_seq, KV_LORA), st[5].reshape(depth, dec_batch, dec_seq, ROPE_DIM),
            st[6].reshape(depth, dec_batch, dec_seq, H_BAND, D_BAND),
            st[7].reshape(depth, dec_batch, dec_seq, H_BAND, D_BAND))
```

```python
import functools

import jax
import jax.numpy as jnp
import numpy as np
from jax import lax
from jax.experimental import pallas as pl
from jax.experimental.pallas import tpu as pltpu

D_MODEL = 2048
CHUNK = 64
H_MLA = 8
NOPE_DIM = 128
ROPE_DIM = 64
HALF_ROPE = ROPE_DIM // 2
V_DIM = 128
Q_LORA = 512
KV_LORA = 256
ROPE_THETA = 10000.0
H_BAND = 8
D_BAND = 128
LEFT_CHUNKS = 8
BAND_PAST = LEFT_CHUNKS * CHUNK
REL_CLIP = 128
MLA_WIDTH = H_MLA * V_DIM
BAND_WIDTH = H_BAND * D_BAND
QK_PAD = 256
N_EXPERTS = 16
N_GROUPS = 4
GROUP_SIZE = 4
EXPERT_FF = 512
RMS_EPS = 1e-6
LN_EPS = 1e-5
NEG = -1e30
LOG2E = 1.4426950408889634
LANE = 128
VMEM_LIMIT = 56 * 1024 * 1024

BF16_ROWS = 16
VT_ROWS = V_DIM + BF16_ROWS
ROW_UNROLL = 8

F32 = jnp.float32
BF16 = jnp.bfloat16


def _cparams(sem):
    return pltpu.CompilerParams(dimension_semantics=sem, vmem_limit_bytes=VMEM_LIMIT)


def _dot(a, b):
    return jnp.dot(a, b, preferred_element_type=F32)


def _dot_nt(a, b):
    return lax.dot_general(a, b, (((1,), (1,)), ((), ())), preferred_element_type=F32)


def _rms(x, g):
    return x * lax.rsqrt(jnp.mean(x * x, axis=-1, keepdims=True) + RMS_EPS) * g


def _layer_norm(x, g, b):
    mu = jnp.mean(x, axis=-1, keepdims=True)
    xc = x - mu
    var = jnp.mean(xc * xc, axis=-1, keepdims=True)
    return xc * lax.rsqrt(var + LN_EPS) * g + b


def _swap32(x):
    w = x.shape[-1]
    lane = lax.broadcasted_iota(jnp.int32, x.shape, x.ndim - 1)
    return jnp.where((lane % ROPE_DIM) < HALF_ROPE, pltpu.roll(x, w - HALF_ROPE, x.ndim - 1),
                     pltpu.roll(x, HALF_ROPE, x.ndim - 1))


def _mla_proj_kernel(x_ref, wa_ref, gq_ref, wq_ref, gkv_ref, wk_ref, wv_ref, c_ref, s_ref,
                     q_ref, k_ref, v_ref, lat_ref, kpe_ref):
    xb = x_ref[...].astype(BF16)
    h = _dot(xb, wa_ref[...])
    cqn = _rms(h[:, :Q_LORA], gq_ref[...])
    lat = _rms(h[:, Q_LORA:Q_LORA + KV_LORA], gkv_ref[...])
    kp = h[:, Q_LORA + KV_LORA:]
    cos = c_ref[...]
    sin = s_ref[...]
    kpr = kp * cos + pltpu.roll(kp, 2 * HALF_ROPE, 1) * sin
    lat_ref[...] = lat
    kpe_ref[...] = (kpr + pltpu.roll(kpr, LANE - HALF_ROPE, 1))[:, :ROPE_DIM]
    kprb = kpr.astype(BF16)
    latb = lat.astype(BF16)
    q = _dot(cqn.astype(BF16), wq_ref[...])
    kn = _dot(latb, wk_ref[...])
    vt = _dot_nt(wv_ref[...], latb).astype(BF16)
    ones = jnp.ones((BF16_ROWS, vt.shape[1]), BF16)
    for hd in range(H_MLA):
        v_ref[hd * VT_ROWS:hd * VT_ROWS + V_DIM, :] = vt[hd * V_DIM:(hd + 1) * V_DIM, :]
        v_ref[hd * VT_ROWS + V_DIM:(hd + 1) * VT_ROWS, :] = ones
    for hd in range(H_MLA):
        a = hd * QK_PAD
        qv = q[:, a + NOPE_DIM:a + QK_PAD]
        q_ref[:, a:a + NOPE_DIM] = q[:, a:a + NOPE_DIM].astype(BF16)
        q_ref[:, a + NOPE_DIM:a + QK_PAD] = (qv * cos + pltpu.roll(qv, 2 * HALF_ROPE, 1) * sin).astype(BF16)
        k_ref[:, a:a + NOPE_DIM] = kn[:, a:a + NOPE_DIM].astype(BF16)
        k_ref[:, a + NOPE_DIM:a + QK_PAD] = kprb


def _mla_proj(x, wa, gq, wq, gkv, wk, wv, cos, sin, seq, tm, tk):
    n = x.shape[0]
    tpb = seq // tm
    per_kv = tk // tm
    const = lambda i: (0, 0)
    row = lambda i: (i, 0)
    return pl.pallas_call(
        _mla_proj_kernel,
        grid=(n // tm,),
        in_specs=[
            pl.BlockSpec((tm, D_MODEL), row),
            pl.BlockSpec(wa.shape, const),
            pl.BlockSpec(gq.shape, const),
            pl.BlockSpec(wq.shape, const),
            pl.BlockSpec(gkv.shape, const),
            pl.BlockSpec(wk.shape, const),
            pl.BlockSpec(wv.shape, const),
            pl.BlockSpec((tm, LANE), lambda i: (i % tpb, 0)),
            pl.BlockSpec((tm, LANE), lambda i: (i % tpb, 0)),
        ],
        out_specs=[
            pl.BlockSpec((tm, H_MLA * QK_PAD), row),
            pl.BlockSpec((tm, H_MLA * QK_PAD), row),
            pl.BlockSpec((None, H_MLA * VT_ROWS, tm), lambda i: (i // per_kv, 0, i % per_kv)),
            pl.BlockSpec((tm, KV_LORA), row),
            pl.BlockSpec((tm, ROPE_DIM), row),
        ],
        out_shape=[
            jax.ShapeDtypeStruct((n, H_MLA * QK_PAD), BF16),
            jax.ShapeDtypeStruct((n, H_MLA * QK_PAD), BF16),
            jax.ShapeDtypeStruct((n // tk, H_MLA * VT_ROWS, tk), BF16),
            jax.ShapeDtypeStruct((n, KV_LORA), F32),
            jax.ShapeDtypeStruct((n, ROPE_DIM), F32),
        ],
        compiler_params=_cparams(("parallel",)),
        name="mla_proj",
    )(x, wa, gq, wq, gkv, wk, wv, cos, sin)


def _band_proj_kernel(x_ref, w_ref, wvt_ref, q_ref, k_ref, v_ref, ks_ref, vs_ref, *, tpb, ntail):
    xb = x_ref[...].astype(BF16)
    q_ref[...] = _dot(xb, w_ref[:, :BAND_WIDTH]).astype(BF16)
    k = _dot(xb, w_ref[:, BAND_WIDTH:2 * BAND_WIDTH])
    k_ref[...] = k.astype(BF16)
    vt = _dot_nt(wvt_ref[...], xb).astype(BF16)
    ones = jnp.ones((BF16_ROWS, vt.shape[1]), BF16)
    for hd in range(H_BAND):
        v_ref[hd * VT_ROWS:hd * VT_ROWS + D_BAND, :] = vt[hd * D_BAND:(hd + 1) * D_BAND, :]
        v_ref[hd * VT_ROWS + D_BAND:(hd + 1) * VT_ROWS, :] = ones

    @pl.when(pl.program_id(0) % tpb >= tpb - ntail)
    def _():
        ks_ref[...] = k
        vs_ref[...] = _dot(xb, w_ref[:, 2 * BAND_WIDTH:])


def _band_proj(x, w, wvt, seq, tm):
    n = x.shape[0]
    tpb = seq // tm
    ntail = BAND_PAST // tm
    batch = n // seq
    const = lambda i: (0, 0)
    row = lambda i: (i, 0)
    tail = lambda i: (i // tpb, jnp.maximum(i % tpb - (tpb - ntail), 0), 0)
    return pl.pallas_call(
        functools.partial(_band_proj_kernel, tpb=tpb, ntail=ntail),
        grid=(n // tm,),
        in_specs=[pl.BlockSpec((tm, D_MODEL), row), pl.BlockSpec(w.shape, const), pl.BlockSpec(wvt.shape, const)],
        out_specs=[
            pl.BlockSpec((tm, BAND_WIDTH), row),
            pl.BlockSpec((tm, BAND_WIDTH), row),
            pl.BlockSpec((None, H_BAND * VT_ROWS, tm), lambda i: (i // ntail, 0, i % ntail)),
            pl.BlockSpec((None, tm, BAND_WIDTH), tail),
            pl.BlockSpec((None, tm, BAND_WIDTH), tail),
        ],
        out_shape=[
            jax.ShapeDtypeStruct((n, BAND_WIDTH), BF16),
            jax.ShapeDtypeStruct((n, BAND_WIDTH), BF16),
            jax.ShapeDtypeStruct((n // BAND_PAST, H_BAND * VT_ROWS, BAND_PAST), BF16),
            jax.ShapeDtypeStruct((batch, BAND_PAST, BAND_WIDTH), F32),
            jax.ShapeDtypeStruct((batch, BAND_PAST, BAND_WIDTH), F32),
        ],
        compiler_params=_cparams(("arbitrary",)),
        name="band_proj",
    )(x, w, wvt)


def _attn_out(acc):
    return (acc[:V_DIM, :] / acc[V_DIM:V_DIM + 1, :]).T


def _mla_attn_kernel(qt_ref, jt_ref, q_ref, k_ref, vt_ref, o_ref, m_ref, acc_ref, sa_ref, sb_ref, *, tq, nblk):
    kc = lax.broadcasted_iota(jnp.int32, (tq, tq), 0) // CHUNK
    qc = lax.broadcasted_iota(jnp.int32, (tq, tq), 1) // CHUNK
    causal = kc <= qc

    def scores(t):
        t = jnp.minimum(t, nblk - 1)
        qo = pl.multiple_of(qt_ref[t] * tq, tq)
        ko = pl.multiple_of(jt_ref[t] * tq, tq)
        return _dot_nt(k_ref[pl.ds(ko, tq), :], q_ref[pl.ds(qo, tq), :])

    def fold(j, st):
        m = m_ref[...]
        m_new = jnp.maximum(m, jnp.max(st, axis=0, keepdims=True))
        p = jnp.exp2(st - m_new)
        acc = jnp.exp2(m - m_new) * acc_ref[...] + _dot(vt_ref[j], p.astype(BF16))
        m_ref[...] = m_new
        return acc

    def step(t, cur_ref, next_ref):
        qi = qt_ref[t]
        j = jt_ref[t]

        @pl.when(j < qi)
        def _():
            next_ref[...] = scores(t + 1)
            acc_ref[...] = fold(j, cur_ref[...])

        @pl.when(j == qi)
        def _():
            next_ref[...] = scores(t + 1)
            acc = fold(j, jnp.where(causal, cur_ref[...], NEG))
            qo = pl.multiple_of(qi * tq, tq)
            o_ref[pl.ds(qo, tq), :] = _attn_out(acc).astype(o_ref.dtype)
            m_ref[...] = jnp.full(m_ref.shape, NEG, F32)
            acc_ref[...] = jnp.zeros(acc_ref.shape, F32)

    m_ref[...] = jnp.full(m_ref.shape, NEG, F32)
    acc_ref[...] = jnp.zeros(acc_ref.shape, F32)
    sa_ref[...] = scores(0)

    def pair(i, c):
        step(2 * i, sa_ref, sb_ref)
        step(2 * i + 1, sb_ref, sa_ref)
        return c

    lax.fori_loop(0, nblk // 2, pair, 0)
    if nblk % 2:
        step(nblk - 1, sa_ref, sb_ref)


def _mla_attn(q, k, vt, seq, tq):
    n = q.shape[0]
    batch = n // seq
    nq = seq // tq
    pairs = [(i, j) for i in range(nq) for j in range(i + 1)]
    qt = jnp.asarray([p[0] for p in pairs], jnp.int32)
    jt = jnp.asarray([p[1] for p in pairs], jnp.int32)
    blk = lambda b, h, qt, jt: (b, h)
    return pl.pallas_call(
        functools.partial(_mla_attn_kernel, tq=tq, nblk=len(pairs)),
        grid_spec=pltpu.PrefetchScalarGridSpec(
            num_scalar_prefetch=2,
            grid=(batch, H_MLA),
            in_specs=[
                pl.BlockSpec((seq, QK_PAD), blk),
                pl.BlockSpec((seq, QK_PAD), blk),
                pl.BlockSpec((nq, VT_ROWS, tq), lambda b, h, qt, jt: (b, h, 0)),
            ],
            out_specs=pl.BlockSpec((seq, V_DIM), blk),
            scratch_shapes=[pltpu.VMEM((1, tq), F32), pltpu.VMEM((VT_ROWS, tq), F32),
                            pltpu.VMEM((tq, tq), F32), pltpu.VMEM((tq, tq), F32)],
        ),
        out_shape=jax.ShapeDtypeStruct((n, MLA_WIDTH), BF16),
        compiler_params=_cparams(("parallel", "parallel")),
        name="mla_attn",
    )(qt, jt, q, k, vt)


def _band_attn_kernel(q_ref, k_ref, vt_ref, b_ref, o_ref, *, nq):
    tq = BAND_PAST
    for i in range(nq):
        q = q_ref[i * tq:(i + 1) * tq, :]
        sc = _dot_nt(k_ref[i * tq:(i + 1) * tq, :], q) + b_ref[tq:, :]
        m = jnp.max(sc, axis=0, keepdims=True)
        if i > 0:
            sp = _dot_nt(k_ref[(i - 1) * tq:i * tq, :], q) + b_ref[:tq, :]
            m = jnp.maximum(m, jnp.max(sp, axis=0, keepdims=True))
            acc = _dot(vt_ref[i - 1], jnp.exp2(sp - m).astype(BF16))
            acc = acc + _dot(vt_ref[i], jnp.exp2(sc - m).astype(BF16))
        else:
            acc = _dot(vt_ref[i], jnp.exp2(sc - m).astype(BF16))
        o_ref[i * tq:(i + 1) * tq, :] = _attn_out(acc).astype(o_ref.dtype)


def _band_attn(q, k, vt, bias_t, seq):
    n = q.shape[0]
    batch = n // seq
    nq = seq // BAND_PAST
    blk = lambda b, h: (b, h)
    return pl.pallas_call(
        functools.partial(_band_attn_kernel, nq=nq),
        grid=(batch, H_BAND),
        in_specs=[
            pl.BlockSpec((seq, D_BAND), blk),
            pl.BlockSpec((seq, D_BAND), blk),
            pl.BlockSpec((nq, VT_ROWS, BAND_PAST), lambda b, h: (b, h, 0)),
            pl.BlockSpec((None, 2 * BAND_PAST, BAND_PAST), lambda b, h: (h, 0, 0)),
        ],
        out_specs=pl.BlockSpec((seq, D_BAND), blk),
        out_shape=jax.ShapeDtypeStruct((n, BAND_WIDTH), BF16),
        compiler_params=_cparams(("parallel", "parallel")),
        name="band_attn",
    )(q, k, vt, bias_t)


def _top2_of4(v, sv):
    best, bi, bs = v[0], jnp.zeros(v[0].shape, jnp.int32), sv[0]
    for i in range(1, GROUP_SIZE):
        c = v[i] > best
        best = jnp.where(c, v[i], best)
        bi = jnp.where(c, i, bi)
        bs = jnp.where(c, sv[i], bs)
    sec = jnp.full(v[0].shape, -jnp.inf, F32)
    si = jnp.zeros(v[0].shape, jnp.int32)
    ss = jnp.zeros(v[0].shape, F32)
    for i in range(GROUP_SIZE):
        c = (bi != i) & (v[i] > sec)
        sec = jnp.where(c, v[i], sec)
        si = jnp.where(c, i, si)
        ss = jnp.where(c, sv[i], ss)
    return best + sec, bi, si, bs, ss


def _route(sel, s):
    groups = []
    for g in range(N_GROUPS):
        rows = range(g * GROUP_SIZE, (g + 1) * GROUP_SIZE)
        groups.append(_top2_of4([sel[i:i + 1, :] for i in rows], [s[i:i + 1, :] for i in rows]))
    score, bi, si, bs, ss = groups[0]
    gi = jnp.zeros(score.shape, jnp.int32)
    for g in range(1, N_GROUPS):
        c = groups[g][0] > score
        score = jnp.where(c, groups[g][0], score)
        gi = jnp.where(c, g, gi)
        bi = jnp.where(c, groups[g][1], bi)
        si = jnp.where(c, groups[g][2], si)
        bs = jnp.where(c, groups[g][3], bs)
        ss = jnp.where(c, groups[g][4], ss)
    den = bs + ss
    return gi * GROUP_SIZE + bi, gi * GROUP_SIZE + si, bs / den, ss / den


def _out_route_kernel(oa_ref, ob_ref, ga_ref, gb_ref, wo_ref, x_ref, lg_ref, lb_ref, wr_ref, br_ref, u_ref,
                      x1_ref, e_ref, r_ref, g_ref, cnt_ref, base_ref, *, alpha):
    @pl.when(pl.program_id(0) == 0)
    def _():
        base_ref[...] = jnp.zeros_like(base_ref)

    na = _rms(oa_ref[...].astype(F32), ga_ref[...]).astype(BF16)
    nb = _rms(ob_ref[...].astype(F32), gb_ref[...]).astype(BF16)
    mix = _dot(na, wo_ref[:MLA_WIDTH, :]) + _dot(nb, wo_ref[MLA_WIDTH:, :])
    x1 = _layer_norm(alpha * x_ref[...] + mix, lg_ref[...], lb_ref[...])
    x1_ref[...] = x1
    logits = _dot_nt(wr_ref[...], x1.astype(BF16))
    s = 1.0 / (1.0 + jnp.exp(-logits))
    e0, e1, g0, g1 = _route(s + br_ref[...], s)
    e_ref[0:1, :] = e0
    e_ref[1:2, :] = e1
    g_ref[0:1, :] = g0
    g_ref[1:2, :] = g1
    eid = lax.broadcasted_iota(jnp.int32, logits.shape, 0)
    oh0 = eid == e0
    oh1 = eid == e1
    both = jnp.where(oh0 | oh1, 1.0, 0.0)
    before = _dot(both.astype(BF16), u_ref[...]) + base_ref[...]
    r_ref[0:1, :] = jnp.sum(jnp.where(oh0, before, 0.0), axis=0, keepdims=True).astype(jnp.int32)
    r_ref[1:2, :] = jnp.sum(jnp.where(oh1, before, 0.0), axis=0, keepdims=True).astype(jnp.int32)
    base = base_ref[...] + jnp.sum(both, axis=1, keepdims=True)
    base_ref[...] = base
    cnt_ref[...] = base


def _out_route(oa, ob, ga, gb, wo, x, lg, lb, wr, br, alpha, tm):
    n = x.shape[0]
    u = (lax.broadcasted_iota(jnp.int32, (tm, tm), 0) < lax.broadcasted_iota(jnp.int32, (tm, tm), 1)).astype(BF16)
    const = lambda i: (0, 0)
    row = lambda i: (i, 0)
    col = lambda i: (0, i)
    return pl.pallas_call(
        functools.partial(_out_route_kernel, alpha=alpha),
        grid=(n // tm,),
        in_specs=[
            pl.BlockSpec((tm, MLA_WIDTH), row),
            pl.BlockSpec((tm, BAND_WIDTH), row),
            pl.BlockSpec(ga.shape, const),
            pl.BlockSpec(gb.shape, const),
            pl.BlockSpec(wo.shape, const),
            pl.BlockSpec((tm, D_MODEL), row),
            pl.BlockSpec(lg.shape, const),
            pl.BlockSpec(lb.shape, const),
            pl.BlockSpec(wr.shape, const),
            pl.BlockSpec(br.shape, const),
            pl.BlockSpec((tm, tm), const),
        ],
        out_specs=[
            pl.BlockSpec((tm, D_MODEL), row),
            pl.BlockSpec((2, tm), col),
            pl.BlockSpec((2, tm), col),
            pl.BlockSpec((2, tm), col),
            pl.BlockSpec((N_EXPERTS, 1), const),
        ],
        out_shape=[
            jax.ShapeDtypeStruct((n, D_MODEL), F32),
            jax.ShapeDtypeStruct((2, n), jnp.int32),
            jax.ShapeDtypeStruct((2, n), jnp.int32),
            jax.ShapeDtypeStruct((2, n), F32),
            jax.ShapeDtypeStruct((N_EXPERTS, 1), F32),
        ],
        scratch_shapes=[pltpu.VMEM((N_EXPERTS, 1), F32)],
        compiler_params=_cparams(("arbitrary",)),
        name="out_route",
    )(oa, ob, ga, gb, wo, x, lg, lb, wr, br, u)


def _slots_kernel(cnt_ref, e_ref, r_ref, d_ref, blk_ref, ex_ref, lo_ref, hi_ref, *, tb, nblocks):
    shift = tb.bit_length() - 1
    e = e_ref[...]
    dest = r_ref[...]
    item = lax.broadcasted_iota(jnp.int32, blk_ref.shape, 1)
    blk = jnp.zeros(blk_ref.shape, jnp.int32)
    exv = jnp.zeros(blk_ref.shape, jnp.int32)
    lo = jnp.zeros(blk_ref.shape, jnp.int32)
    hi = jnp.zeros(blk_ref.shape, jnp.int32)
    start = jnp.int32(0)
    first = jnp.int32(0)
    last_e = jnp.int32(0)
    for ex in range(N_EXPERTS):
        cnt = cnt_ref[ex]
        end = start + cnt
        sb = lax.shift_right_logical(start, shift)
        nitems = jnp.where(cnt > 0, lax.shift_right_logical(end - 1, shift) - sb + 1, 0)
        dest = dest + jnp.where(e == ex, start, 0)
        mine = (item >= first) & (item < first + nitems)
        b = sb + item - first
        blk = jnp.where(mine, b, blk)
        exv = jnp.where(mine, ex, exv)
        lo = jnp.where(mine, jnp.maximum(start, b * tb) - b * tb, lo)
        hi = jnp.where(mine, jnp.minimum(end, (b + 1) * tb) - b * tb, hi)
        last_e = jnp.where(cnt > 0, ex, last_e)
        start = end
        first = first + nitems
    unused = item >= first
    d_ref[...] = dest
    blk_ref[...] = jnp.where(unused, nblocks - 1, blk)
    ex_ref[...] = jnp.where(unused, last_e, exv)
    lo_ref[...] = lo
    hi_ref[...] = hi


def _slots(cnt, e, r, tb, nwp):
    n = e.shape[1]
    full = lambda shape: pl.BlockSpec(shape, lambda: (0, 0))
    return pl.pallas_call(
        functools.partial(_slots_kernel, tb=tb, nblocks=2 * n // tb),
        in_specs=[pl.BlockSpec(memory_space=pltpu.SMEM), full((2, n)), full((2, n))],
        out_specs=[full((2, n))] + [full((1, nwp))] * 4,
        out_shape=[jax.ShapeDtypeStruct((2, n), jnp.int32)] + [jax.ShapeDtypeStruct((1, nwp), jnp.int32)] * 4,
        name="moe_slots",
    )(cnt, e, r)


def _dispatch_kernel(d_ref, x_ref, xs_ref, sem, *, tm):
    def issue(r, c):
        for k in range(2):
            pltpu.make_async_copy(x_ref.at[pl.ds(r, 1)], xs_ref.at[pl.ds(d_ref[0, 0, k * tm + r], 1)],
                                  sem.at[k]).start(priority=k)
        return c

    lax.fori_loop(0, tm, issue, 0, unroll=ROW_UNROLL)
    for k in range(2):
        pltpu.make_async_copy(x_ref, xs_ref.at[pl.ds(0, tm)], sem.at[k]).wait()


def _dispatch(dest3, x1, nslots, tm):
    n = x1.shape[0]
    return pl.pallas_call(
        functools.partial(_dispatch_kernel, tm=tm),
        grid=(n // tm,),
        in_specs=[
            pl.BlockSpec((1, 1, 2 * tm), lambda i: (i, 0, 0), memory_space=pltpu.SMEM),
            pl.BlockSpec((tm, D_MODEL), lambda i: (i, 0)),
        ],
        out_specs=pl.BlockSpec(memory_space=pl.ANY),
        out_shape=jax.ShapeDtypeStruct((nslots, D_MODEL), F32),
        scratch_shapes=[pltpu.SemaphoreType.DMA((2,))],
        compiler_params=_cparams(("arbitrary",)),
        name="moe_dispatch",
    )(dest3, x1)


def _experts_kernel(blk_ref, ex_ref, lo_ref, hi_ref, xs_ref, wg_ref, wu_ref, wd_ref, y_ref):
    i = pl.program_id(0)
    lo = lo_ref[i]
    hi = hi_ref[i]
    opens_block = (i == 0) | (blk_ref[i] != blk_ref[jnp.maximum(i - 1, 0)])

    @pl.when(lo < hi)
    def _():
        xb = xs_ref[...].astype(BF16)
        g = _dot(xb, wg_ref[...])
        u = _dot(xb, wu_ref[...])
        hid = (g / (1.0 + jnp.exp(-g))) * u
        y = _dot(hid.astype(BF16), wd_ref[...])
        row = lax.broadcasted_iota(jnp.int32, (y.shape[0], 1), 0)
        y = jnp.where((row >= lo) & (row < hi), y, 0.0)

        @pl.when(opens_block)
        def _():
            y_ref[...] = y

        @pl.when(jnp.logical_not(opens_block))
        def _():
            y_ref[...] += y


def _experts(blk, ex, lo, hi, xs, wg, wu, wd, layer, tb, nwork):
    nslots = xs.shape[0]
    rows = lambda i, blk, ex, lo, hi: (blk[i], 0)
    wmap = lambda i, blk, ex, lo, hi: (layer, ex[i], 0, 0)
    return pl.pallas_call(
        _experts_kernel,
        grid_spec=pltpu.PrefetchScalarGridSpec(
            num_scalar_prefetch=4,
            grid=(nwork,),
            in_specs=[
                pl.BlockSpec((tb, D_MODEL), rows),
                pl.BlockSpec((None, None, D_MODEL, EXPERT_FF), wmap),
                pl.BlockSpec((None, None, D_MODEL, EXPERT_FF), wmap),
                pl.BlockSpec((None, None, EXPERT_FF, D_MODEL), wmap),
            ],
            out_specs=pl.BlockSpec((tb, D_MODEL), rows),
        ),
        out_shape=jax.ShapeDtypeStruct((nslots, D_MODEL), F32),
        compiler_params=_cparams(("arbitrary",)),
        name="moe_experts",
    )(blk, ex, lo, hi, xs, wg, wu, wd)


def _combine_kernel(dc_ref, dn_ref, x_ref, gt_ref, lg_ref, lb_ref, y_ref, o_ref, buf, sem, *, tm, alpha):
    i = pl.program_id(0)
    slot = i % 2

    def gather(d_ref, s):
        def issue(r, c):
            for k in range(2):
                pltpu.make_async_copy(y_ref.at[pl.ds(d_ref[0, 0, k * tm + r], 1)], buf.at[2 * s + k, pl.ds(r, 1)],
                                      sem.at[s, k]).start(priority=k)
            return c

        lax.fori_loop(0, tm, issue, 0, unroll=ROW_UNROLL)

    @pl.when(i == 0)
    def _():
        gather(dc_ref, 0)

    @pl.when(i + 1 < pl.num_programs(0))
    def _():
        gather(dn_ref, 1 - slot)

    for k in range(2):
        pltpu.make_async_copy(y_ref.at[pl.ds(0, tm)], buf.at[2 * slot + k], sem.at[slot, k]).wait()
    ffn = buf[2 * slot] * gt_ref[:, 0:1] + buf[2 * slot + 1] * gt_ref[:, 1:2]
    o_ref[...] = _layer_norm(alpha * x_ref[...] + ffn, lg_ref[...], lb_ref[...])


def _combine(dest3, x1, gates_t, lg, lb, yb, alpha, tm):
    n = x1.shape[0]
    nt = n // tm
    const = lambda i: (0, 0)
    row = lambda i: (i, 0)
    return pl.pallas_call(
        functools.partial(_combine_kernel, tm=tm, alpha=alpha),
        grid=(nt,),
        in_specs=[
            pl.BlockSpec((1, 1, 2 * tm), lambda i: (i, 0, 0), memory_space=pltpu.SMEM),
            pl.BlockSpec((1, 1, 2 * tm), lambda i: (jnp.minimum(i + 1, nt - 1), 0, 0), memory_space=pltpu.SMEM),
            pl.BlockSpec((tm, D_MODEL), row),
            pl.BlockSpec((tm, 2), row),
            pl.BlockSpec(lg.shape, const),
            pl.BlockSpec(lb.shape, const),
            pl.BlockSpec(memory_space=pl.ANY),
        ],
        out_specs=pl.BlockSpec((tm, D_MODEL), row),
        out_shape=jax.ShapeDtypeStruct((n, D_MODEL), F32),
        scratch_shapes=[pltpu.VMEM((4, tm, D_MODEL), F32), pltpu.SemaphoreType.DMA((2, 2))],
        compiler_params=_cparams(("arbitrary",)),
        name="moe_combine",
    )(dest3, dest3, x1, gates_t, lg, lb, yb)


def _moe_ln(x1, e, r, gates, cnt, wg, wu, wd, layer, lg, lb, alpha, tm, tb):
    n = x1.shape[0]
    nwork = (2 * n) // tb + N_EXPERTS - 1
    nwp = -(-nwork // LANE) * LANE
    dest, blk, ex, lo, hi = _slots(cnt.astype(jnp.int32).reshape(N_EXPERTS), e, r, tb, nwp)
    dest3 = dest.reshape(2, n // tm, tm).transpose(1, 0, 2).reshape(n // tm, 1, 2 * tm)
    xs = _dispatch(dest3, x1, 2 * n, tm)
    yb = _experts(blk.reshape(nwp), ex.reshape(nwp), lo.reshape(nwp), hi.reshape(nwp), xs, wg, wu, wd,
                  layer, tb, nwork)
    return _combine(dest3, x1, gates.T, lg, lb, yb, alpha, tm)


def _sample_proj_kernel(x_ref, w_ref, gq_ref, wq_ref, gkv_ref, c_ref, s_ref,
                        q_ref, lat_ref, kpe_ref, qb_ref, kb_ref, vb_ref, acc_ref):
    k = pl.program_id(0)

    @pl.when(k == 0)
    def _():
        acc_ref[...] = jnp.zeros_like(acc_ref)

    acc_ref[...] += _dot(x_ref[...].astype(BF16), w_ref[...].astype(BF16))

    @pl.when(k == pl.num_programs(0) - 1)
    def _():
        cos = c_ref[...]
        sin = s_ref[...]
        cqn = _rms(acc_ref[:, :Q_LORA], gq_ref[...])
        q = _dot(cqn.astype(BF16), wq_ref[...])
        q_ref[:, :H_MLA * NOPE_DIM] = q[:, :H_MLA * NOPE_DIM]
        qr = q[:, H_MLA * NOPE_DIM:]
        q_ref[:, H_MLA * NOPE_DIM:] = qr * cos + _swap32(qr) * sin
        lat_ref[...] = _rms(acc_ref[:, Q_LORA:Q_LORA + KV_LORA], gkv_ref[...])
        a = Q_LORA + KV_LORA
        kv = acc_ref[:, a:a + LANE]
        kpe_ref[...] = (kv * cos[:, :LANE] + _swap32(kv) * sin[:, :LANE])[:, :ROPE_DIM]
        a += ROPE_DIM
        qb_ref[...] = acc_ref[:, a:a + BAND_WIDTH]
        kb_ref[...] = acc_ref[:, a + BAND_WIDTH:a + 2 * BAND_WIDTH]
        vb_ref[...] = acc_ref[:, a + 2 * BAND_WIDTH:a + 3 * BAND_WIDTH]


def _sample_proj(x, w_in, layer, gq, wq, gkv, cos, sin, tk):
    m = x.shape[0]
    width = w_in.shape[2]
    const = lambda k: (0, 0)
    shapes = [(m, wq.shape[1]), (m, KV_LORA), (m, ROPE_DIM), (m, BAND_WIDTH), (m, BAND_WIDTH), (m, BAND_WIDTH)]
    return pl.pallas_call(
        _sample_proj_kernel,
        grid=(D_MODEL // tk,),
        in_specs=[
            pl.BlockSpec((m, tk), lambda k: (0, k)),
            pl.BlockSpec((None, tk, width), lambda k: (layer, k, 0)),
            pl.BlockSpec(gq.shape, const),
            pl.BlockSpec(wq.shape, const),
            pl.BlockSpec(gkv.shape, const),
            pl.BlockSpec(cos.shape, const),
            pl.BlockSpec(sin.shape, const),
        ],
        out_specs=[pl.BlockSpec(s, const) for s in shapes],
        out_shape=[jax.ShapeDtypeStruct(s, F32) for s in shapes],
        scratch_shapes=[pltpu.VMEM((m, width), F32)],
        compiler_params=_cparams(("arbitrary",)),
        name="sample_proj",
    )(x, w_in, gq, wq, gkv, cos, sin)


def _softmax2(sc, sn):
    m = jnp.maximum(jnp.max(sc, axis=-1, keepdims=True), jnp.max(sn, axis=-1, keepdims=True))
    ec = jnp.exp(sc - m)
    en = jnp.exp(sn - m)
    l = jnp.sum(ec, axis=-1, keepdims=True) + jnp.sum(en, axis=-1, keepdims=True)
    return (ec / l).astype(BF16), (en / l).astype(BF16)


def _sample_mla_kernel(q_ref, lat_ref, kpe_ref, cl_ref, cr_ref, wk_ref, wv_ref, o_ref, *, scale):
    lat_c = cl_ref[...].astype(BF16)
    lat_n = lat_ref[...].astype(BF16)
    kpe_c = cr_ref[...].astype(BF16)
    kpe_n = kpe_ref[...].astype(BF16)
    kc = _dot(lat_c, wk_ref[...]).astype(BF16)
    vc = _dot(lat_c, wv_ref[...]).astype(BF16)
    kn = _dot(lat_n, wk_ref[...]).astype(BF16)
    vn = _dot(lat_n, wv_ref[...]).astype(BF16)
    for hd in range(H_MLA):
        a = hd * NOPE_DIM
        b = H_MLA * NOPE_DIM + hd * ROPE_DIM
        qn = q_ref[:, a:a + NOPE_DIM].astype(BF16)
        qr = q_ref[:, b:b + ROPE_DIM].astype(BF16)
        sc = (_dot_nt(qn, kc[:, a:a + NOPE_DIM]) + _dot_nt(qr, kpe_c)) * scale
        sn = (_dot_nt(qn, kn[:, a:a + NOPE_DIM]) + _dot_nt(qr, kpe_n)) * scale
        pc, pn = _softmax2(sc, sn)
        o_ref[:, a:a + V_DIM] = _dot(pc, vc[:, a:a + V_DIM]) + _dot(pn, vn[:, a:a + V_DIM])


def _sample_mla(q, lat, kpe, cache_lat, cache_rope, wk, wv, layer, t):
    m = q.shape[0]
    batch = m // t
    past = cache_lat.shape[2]
    row = lambda b: (b, 0)
    const = lambda b: (0, 0)
    return pl.pallas_call(
        functools.partial(_sample_mla_kernel, scale=(NOPE_DIM + ROPE_DIM) ** -0.5),
        grid=(batch,),
        in_specs=[
            pl.BlockSpec((t, q.shape[1]), row),
            pl.BlockSpec((t, KV_LORA), row),
            pl.BlockSpec((t, ROPE_DIM), row),
            pl.BlockSpec((None, None, past, KV_LORA), lambda b: (layer, b, 0, 0)),
            pl.BlockSpec((None, None, past, ROPE_DIM), lambda b: (layer, b, 0, 0)),
            pl.BlockSpec((None, KV_LORA, MLA_WIDTH), lambda b: (layer, 0, 0)),
            pl.BlockSpec((None, KV_LORA, MLA_WIDTH), lambda b: (layer, 0, 0)),
        ],
        out_specs=pl.BlockSpec((t, MLA_WIDTH), row),
        out_shape=jax.ShapeDtypeStruct((m, MLA_WIDTH), F32),
        compiler_params=_cparams(("parallel",)),
        name="sample_mla",
    )(q, lat, kpe, cache_lat, cache_rope, wk, wv)


def _sample_band_kernel(q_ref, kn_ref, vn_ref, ck_ref, cv_ref, bc_ref, bn_ref, o_ref, *, scale):
    for hd in range(H_BAND):
        a = hd * D_BAND
        q = q_ref[:, a:a + D_BAND].astype(BF16)
        sc = _dot_nt(q, ck_ref[:, a:a + D_BAND].astype(BF16)) * scale + bc_ref[hd]
        sn = _dot_nt(q, kn_ref[:, a:a + D_BAND].astype(BF16)) * scale + bn_ref[hd]
        pc, pn = _softmax2(sc, sn)
        o_ref[:, a:a + D_BAND] = (_dot(pc, cv_ref[:, a:a + D_BAND].astype(BF16))
                                  + _dot(pn, vn_ref[:, a:a + D_BAND].astype(BF16)))


def _sample_band(q, kn, vn, cache_k, cache_v, bias_c, bias_n, layer, t):
    m = q.shape[0]
    batch = m // t
    r = cache_k.shape[2]
    row = lambda b: (b, 0)
    cache = lambda b: (layer, b, 0, 0)
    return pl.pallas_call(
        functools.partial(_sample_band_kernel, scale=D_BAND ** -0.5),
        grid=(batch,),
        in_specs=[
            pl.BlockSpec((t, BAND_WIDTH), row),
            pl.BlockSpec((t, BAND_WIDTH), row),
            pl.BlockSpec((t, BAND_WIDTH), row),
            pl.BlockSpec((None, None, r, BAND_WIDTH), cache),
            pl.BlockSpec((None, None, r, BAND_WIDTH), cache),
            pl.BlockSpec(bias_c.shape, lambda b: (0, 0, 0)),
            pl.BlockSpec(bias_n.shape, lambda b: (0, 0, 0)),
        ],
        out_specs=pl.BlockSpec((t, BAND_WIDTH), row),
        out_shape=jax.ShapeDtypeStruct((m, BAND_WIDTH), F32),
        compiler_params=_cparams(("parallel",)),
        name="sample_band",
    )(q, kn, vn, cache_k, cache_v, bias_c, bias_n)


def _spread(w):
    z = jnp.zeros(w.shape[:-1] + (HALF_ROPE,), w.dtype)
    return jnp.concatenate([w[..., :HALF_ROPE], z, w[..., HALF_ROPE:], z], axis=-1)


def _rope_angles(pos):
    inv = 1.0 / (ROPE_THETA ** (jnp.arange(0, ROPE_DIM, 2, dtype=F32) / ROPE_DIM))
    ang = pos.astype(F32)[:, None] * inv[None, :]
    return jnp.cos(ang), jnp.sin(ang)


def _band_bias_kernel(line_ref, o_ref, *, q0, k0, keys_first):
    nr, nc = o_ref.shape
    table = pltpu.roll(jnp.broadcast_to(line_ref[...], (nr, line_ref.shape[1])), 0, 1, stride=1, stride_axis=0)
    row = lax.broadcasted_iota(jnp.int32, (nr, nc), 0)
    col = lax.broadcasted_iota(jnp.int32, (nr, nc), 1)
    shift = CHUNK * (1 << 20)
    kc = ((row if keys_first else col) + (k0 + shift)) // CHUNK
    qc = ((col if keys_first else row) + (q0 + shift)) // CHUNK
    o_ref[...] = jnp.where((kc <= qc) & (kc >= qc - LEFT_CHUNKS), table[:, :nc], NEG)


def _band_bias_table(rel_bias, q0, nq, k0, nk, keys_first):
    heads = rel_bias.shape[0]
    nr, nc = (nk, nq) if keys_first else (nq, nk)
    period = -(-(nr + nc) // LANE) * LANE
    delta = np.arange(period)
    delta = np.where(delta < nc, delta, delta - period)
    qk = (q0 - k0) + (delta if keys_first else -delta)
    line = rel_bias[:, np.clip(qk, -REL_CLIP, REL_CLIP) + REL_CLIP].reshape(heads, 1, period)
    return pl.pallas_call(
        functools.partial(_band_bias_kernel, q0=q0, k0=k0, keys_first=keys_first),
        grid=(heads,),
        in_specs=[pl.BlockSpec((None, 1, period), lambda h: (h, 0, 0))],
        out_specs=pl.BlockSpec((None, nr, nc), lambda h: (h, 0, 0)),
        out_shape=jax.ShapeDtypeStruct((heads, nr, nc), F32),
        compiler_params=_cparams(("parallel",)),
        name="band_bias",
    )(line)


def kernel(x_prompt, x_sample, cache_mla_latent, cache_mla_rope, cache_band_k, cache_band_v, w_in, g_q_lat, w_q_up, g_kv_lat, w_k_up, w_v_up, rel_bias, g_out_mla, g_out_band, w_out, ln1_g, ln1_b, w_router, b_router, w_gate, w_up, w_down, ln2_g, ln2_b):
    depth = w_in.shape[0]
    batch, seq, _ = x_prompt.shape
    dec_batch, dec_seq, _ = x_sample.shape
    past = cache_mla_latent.shape[2]
    band_r = cache_band_k.shape[2]
    alpha = float((2 * depth) ** 0.25)
    n = batch * seq
    m = dec_batch * dec_seq
    mla_scale = (NOPE_DIM + ROPE_DIM) ** -0.5 * LOG2E
    band_scale = D_BAND ** -0.5 * LOG2E
    a0 = Q_LORA + KV_LORA
    a1 = a0 + ROPE_DIM

    w_a = jnp.concatenate([w_in[:, :, :a0], _spread(w_in[:, :, a0:a1])], axis=-1).astype(BF16)
    w_band = jnp.concatenate([w_in[:, :, a1:a1 + BAND_WIDTH] * band_scale, w_in[:, :, a1 + BAND_WIDTH:]],
                             axis=-1).astype(BF16)
    w_band_vt = w_in[:, :, a1 + 2 * BAND_WIDTH:].transpose(0, 2, 1).astype(BF16)
    wq4 = w_q_up.reshape(depth, Q_LORA, H_MLA, NOPE_DIM + ROPE_DIM)
    wq_pad = jnp.concatenate([wq4[..., :NOPE_DIM], _spread(wq4[..., NOPE_DIM:])], axis=-1) * mla_scale
    wq_pad = wq_pad.reshape(depth, Q_LORA, H_MLA * QK_PAD).astype(BF16)
    wq_s = jnp.concatenate([wq4[..., :NOPE_DIM].reshape(depth, Q_LORA, -1),
                            wq4[..., NOPE_DIM:].reshape(depth, Q_LORA, -1)], axis=-1).astype(BF16)
    wk4 = w_k_up.reshape(depth, KV_LORA, H_MLA, NOPE_DIM)
    wk_pad = jnp.concatenate([wk4, jnp.zeros_like(wk4)], axis=-1).reshape(depth, KV_LORA, H_MLA * QK_PAD)
    wk_pad = wk_pad.astype(BF16)
    wk_b = w_k_up.astype(BF16)
    wv_b = w_v_up.astype(BF16)
    wv_t = w_v_up.transpose(0, 2, 1).astype(BF16)
    w_out_b = w_out.astype(BF16)
    wg_b = w_gate.astype(BF16)
    wu_b = w_up.astype(BF16)
    wd_b = w_down.astype(BF16)
    wr_tb = w_router.T.astype(BF16)
    br = b_router.reshape(N_EXPERTS, 1)
    row2 = lambda a: a.reshape(depth, 1, -1)
    gq, gkv, ga, gb = row2(g_q_lat), row2(g_kv_lat), row2(g_out_mla), row2(g_out_band)
    l1g, l1b, l2g, l2b = row2(ln1_g), row2(ln1_b), row2(ln2_g), row2(ln2_b)

    pos_p = jnp.arange(seq, dtype=jnp.int32)
    cos_p, sin_p = _rope_angles(pos_p)
    z = jnp.zeros_like(cos_p)
    cos_sp = jnp.concatenate([cos_p, z, cos_p, z], axis=-1)
    sin_sp = jnp.concatenate([-sin_p, z, sin_p, z], axis=-1)
    pos_s = past + jnp.arange(dec_seq, dtype=jnp.int32)
    cos_s, sin_s = _rope_angles(pos_s)
    cos_s = jnp.tile(jnp.concatenate([cos_s, cos_s], axis=-1), (dec_batch, H_MLA))
    sin_s = jnp.tile(jnp.concatenate([-sin_s, sin_s], axis=-1), (dec_batch, H_MLA))

    tm_p = 256
    states = [[], [], [], [], [], [], [], []]
    xp = x_prompt.reshape(n, D_MODEL)
    xs = x_sample.reshape(m, D_MODEL)
    ck = cache_band_k.reshape(depth, dec_batch, band_r, BAND_WIDTH)
    cv = cache_band_v.reshape(depth, dec_batch, band_r, BAND_WIDTH)
    for l in range(depth):
        q, k, vt, lat, kpe = _mla_proj(xp, w_a[l], gq[l], wq_pad[l], gkv[l], wk_pad[l], wv_t[l],
                                       cos_sp, sin_sp, seq, tm_p, BAND_PAST)
        qb, kb, vbt, ks, vs = _band_proj(xp, w_band[l], w_band_vt[l], seq, tm_p)
        oa = _mla_attn(q, k, vt, seq, BAND_PAST)
        bias_p = _band_bias_table(rel_bias[l] * LOG2E, 0, BAND_PAST, -BAND_PAST, 2 * BAND_PAST, True)
        ob = _band_attn(qb, kb, vbt, bias_p, seq)
        x1, e, r, gates, cnt = _out_route(oa, ob, ga[l], gb[l], w_out_b[l], xp, l1g[l], l1b[l], wr_tb, br,
                                          alpha, tm_p)
        xp = _moe_ln(x1, e, r, gates, cnt, wg_b, wu_b, wd_b, l, l2g[l], l2b[l], alpha, tm_p, 256)
        for acc, s in zip(states[:4], (lat, kpe, ks, vs)):
            acc.append(s)
        sq, slat, skpe, sqb, skb, svb = _sample_proj(xs, w_in, l, gq[l], wq_s[l], gkv[l], cos_s, sin_s, 256)
        soa = _sample_mla(sq, slat, skpe, cache_mla_latent, cache_mla_rope, wk_b, wv_b, l, dec_seq)
        bias_c = _band_bias_table(rel_bias[l], past, dec_seq, past - band_r, band_r, False)
        bias_n = _band_bias_table(rel_bias[l], past, dec_seq, past, dec_seq, False)
        sob = _sample_band(sqb, skb, svb, ck, cv, bias_c, bias_n, l, dec_seq)
        sx1, se, sr, sgates, scnt = _out_route(soa, sob, ga[l], gb[l], w_out_b[l], xs, l1g[l], l1b[l], wr_tb, br,
                                               alpha, m)
        xs = _moe_ln(sx1, se, sr, sgates, scnt, wg_b, wu_b, wd_b, l, l2g[l], l2b[l], alpha, m, 128)
        for acc, s in zip(states[4:], (slat, skpe, skb, svb)):
            acc.append(s)

    st = [jnp.stack(a) for a in states]
    return (xp.reshape(batch, seq, D_MODEL), xs.reshape(dec_batch, dec_seq, D_MODEL),
            st[0].reshape(depth, batch, seq, KV_LORA), st[1].reshape(depth, batch, seq, ROPE_DIM),
            st[2].reshape(depth, batch, BAND_PAST, H_BAND, D_BAND),
            st[3].reshape(depth, batch, BAND_PAST, H_BAND, D_BAND),
            st[4].reshape(depth, dec_batch, dec_seq, KV_LORA), st[5].reshape(depth, dec_batch, dec_seq, ROPE_DIM),
            st[6].reshape(depth, dec_batch, dec_seq, H_BAND, D_BAND),
            st[7].reshape(depth, dec_batch, dec_seq, H_BAND, D_BAND))
```

```python
import functools

import jax
import jax.numpy as jnp
import numpy as np
from jax import lax
from jax.experimental import pallas as pl
from jax.experimental.pallas import tpu as pltpu

D_MODEL = 2048
CHUNK = 64
H_MLA = 8
NOPE_DIM = 128
ROPE_DIM = 64
HALF_ROPE = ROPE_DIM // 2
V_DIM = 128
Q_LORA = 512
KV_LORA = 256
ROPE_THETA = 10000.0
H_BAND = 8
D_BAND = 128
LEFT_CHUNKS = 8
BAND_PAST = LEFT_CHUNKS * CHUNK
REL_CLIP = 128
MLA_WIDTH = H_MLA * V_DIM
BAND_WIDTH = H_BAND * D_BAND
QK_PAD = 256
N_EXPERTS = 16
N_GROUPS = 4
GROUP_SIZE = 4
EXPERT_FF = 512
RMS_EPS = 1e-6
LN_EPS = 1e-5
NEG = -1e30
LOG2E = 1.4426950408889634
LANE = 128
VMEM_LIMIT = 56 * 1024 * 1024

BF16_ROWS = 16
VT_ROWS = V_DIM + BF16_ROWS
ROW_UNROLL = 8

F32 = jnp.float32
BF16 = jnp.bfloat16


def _cparams(sem):
    return pltpu.CompilerParams(dimension_semantics=sem, vmem_limit_bytes=VMEM_LIMIT)


def _dot(a, b):
    return jnp.dot(a, b, preferred_element_type=F32)


def _dot_nt(a, b):
    return lax.dot_general(a, b, (((1,), (1,)), ((), ())), preferred_element_type=F32)


def _rms(x, g):
    return x * lax.rsqrt(jnp.mean(x * x, axis=-1, keepdims=True) + RMS_EPS) * g


def _layer_norm(x, g, b):
    mu = jnp.mean(x, axis=-1, keepdims=True)
    xc = x - mu
    var = jnp.mean(xc * xc, axis=-1, keepdims=True)
    return xc * lax.rsqrt(var + LN_EPS) * g + b


def _swap32(x):
    w = x.shape[-1]
    lane = lax.broadcasted_iota(jnp.int32, x.shape, x.ndim - 1)
    return jnp.where((lane % ROPE_DIM) < HALF_ROPE, pltpu.roll(x, w - HALF_ROPE, x.ndim - 1),
                     pltpu.roll(x, HALF_ROPE, x.ndim - 1))


def _mla_proj_kernel(x_ref, wa_ref, gq_ref, wq_ref, gkv_ref, wk_ref, wv_ref, c_ref, s_ref,
                     q_ref, k_ref, v_ref, lat_ref, kpe_ref):
    xb = x_ref[...].astype(BF16)
    h = _dot(xb, wa_ref[...])
    cqn = _rms(h[:, :Q_LORA], gq_ref[...])
    lat = _rms(h[:, Q_LORA:Q_LORA + KV_LORA], gkv_ref[...])
    kp = h[:, Q_LORA + KV_LORA:]
    cos = c_ref[...]
    sin = s_ref[...]
    kpr = kp * cos + pltpu.roll(kp, 2 * HALF_ROPE, 1) * sin
    lat_ref[...] = lat
    kpe_ref[...] = (kpr + pltpu.roll(kpr, LANE - HALF_ROPE, 1))[:, :ROPE_DIM]
    kprb = kpr.astype(BF16)
    latb = lat.astype(BF16)
    q = _dot(cqn.astype(BF16), wq_ref[...])
    kn = _dot(latb, wk_ref[...])
    vt = _dot_nt(wv_ref[...], latb).astype(BF16)
    ones = jnp.ones((BF16_ROWS, vt.shape[1]), BF16)
    for hd in range(H_MLA):
        v_ref[hd * VT_ROWS:hd * VT_ROWS + V_DIM, :] = vt[hd * V_DIM:(hd + 1) * V_DIM, :]
        v_ref[hd * VT_ROWS + V_DIM:(hd + 1) * VT_ROWS, :] = ones
    for hd in range(H_MLA):
        a = hd * QK_PAD
        qv = q[:, a + NOPE_DIM:a + QK_PAD]
        q_ref[:, a:a + NOPE_DIM] = q[:, a:a + NOPE_DIM].astype(BF16)
        q_ref[:, a + NOPE_DIM:a + QK_PAD] = (qv * cos + pltpu.roll(qv, 2 * HALF_ROPE, 1) * sin).astype(BF16)
        k_ref[:, a:a + NOPE_DIM] = kn[:, a:a + NOPE_DIM].astype(BF16)
        k_ref[:, a + NOPE_DIM:a + QK_PAD] = kprb


def _mla_proj(x, wa, gq, wq, gkv, wk, wv, cos, sin, seq, tm, tk):
    n = x.shape[0]
    tpb = seq // tm
    per_kv = tk // tm
    const = lambda i: (0, 0)
    row = lambda i: (i, 0)
    return pl.pallas_call(
        _mla_proj_kernel,
        grid=(n // tm,),
        in_specs=[
            pl.BlockSpec((tm, D_MODEL), row),
            pl.BlockSpec(wa.shape, const),
            pl.BlockSpec(gq.shape, const),
            pl.BlockSpec(wq.shape, const),
            pl.BlockSpec(gkv.shape, const),
            pl.BlockSpec(wk.shape, const),
            pl.BlockSpec(wv.shape, const),
            pl.BlockSpec((tm, LANE), lambda i: (i % tpb, 0)),
            pl.BlockSpec((tm, LANE), lambda i: (i % tpb, 0)),
        ],
        out_specs=[
            pl.BlockSpec((tm, H_MLA * QK_PAD), row),
            pl.BlockSpec((tm, H_MLA * QK_PAD), row),
            pl.BlockSpec((None, H_MLA * VT_ROWS, tm), lambda i: (i // per_kv, 0, i % per_kv)),
            pl.BlockSpec((tm, KV_LORA), row),
            pl.BlockSpec((tm, ROPE_DIM), row),
        ],
        out_shape=[
            jax.ShapeDtypeStruct((n, H_MLA * QK_PAD), BF16),
            jax.ShapeDtypeStruct((n, H_MLA * QK_PAD), BF16),
            jax.ShapeDtypeStruct((n // tk, H_MLA * VT_ROWS, tk), BF16),
            jax.ShapeDtypeStruct((n, KV_LORA), F32),
            jax.ShapeDtypeStruct((n, ROPE_DIM), F32),
        ],
        compiler_params=_cparams(("parallel",)),
        name="mla_proj",
    )(x, wa, gq, wq, gkv, wk, wv, cos, sin)


def _band_proj_kernel(x_ref, w_ref, wvt_ref, q_ref, k_ref, v_ref, ks_ref, vs_ref, *, tpb, ntail):
    xb = x_ref[...].astype(BF16)
    q_ref[...] = _dot(xb, w_ref[:, :BAND_WIDTH]).astype(BF16)
    k = _dot(xb, w_ref[:, BAND_WIDTH:2 * BAND_WIDTH])
    k_ref[...] = k.astype(BF16)
    vt = _dot_nt(wvt_ref[...], xb).astype(BF16)
    ones = jnp.ones((BF16_ROWS, vt.shape[1]), BF16)
    for hd in range(H_BAND):
        v_ref[hd * VT_ROWS:hd * VT_ROWS + D_BAND, :] = vt[hd * D_BAND:(hd + 1) * D_BAND, :]
        v_ref[hd * VT_ROWS + D_BAND:(hd + 1) * VT_ROWS, :] = ones

    @pl.when(pl.program_id(0) % tpb >= tpb - ntail)
    def _():
        ks_ref[...] = k
        vs_ref[...] = _dot(xb, w_ref[:, 2 * BAND_WIDTH:])


def _band_proj(x, w, wvt, seq, tm):
    n = x.shape[0]
    tpb = seq // tm
    ntail = BAND_PAST // tm
    batch = n // seq
    const = lambda i: (0, 0)
    row = lambda i: (i, 0)
    tail = lambda i: (i // tpb, jnp.maximum(i % tpb - (tpb - ntail), 0), 0)
    return pl.pallas_call(
        functools.partial(_band_proj_kernel, tpb=tpb, ntail=ntail),
        grid=(n // tm,),
        in_specs=[pl.BlockSpec((tm, D_MODEL), row), pl.BlockSpec(w.shape, const), pl.BlockSpec(wvt.shape, const)],
        out_specs=[
            pl.BlockSpec((tm, BAND_WIDTH), row),
            pl.BlockSpec((tm, BAND_WIDTH), row),
            pl.BlockSpec((None, H_BAND * VT_ROWS, tm), lambda i: (i // ntail, 0, i % ntail)),
            pl.BlockSpec((None, tm, BAND_WIDTH), tail),
            pl.BlockSpec((None, tm, BAND_WIDTH), tail),
        ],
        out_shape=[
            jax.ShapeDtypeStruct((n, BAND_WIDTH), BF16),
            jax.ShapeDtypeStruct((n, BAND_WIDTH), BF16),
            jax.ShapeDtypeStruct((n // BAND_PAST, H_BAND * VT_ROWS, BAND_PAST), BF16),
            jax.ShapeDtypeStruct((batch, BAND_PAST, BAND_WIDTH), F32),
            jax.ShapeDtypeStruct((batch, BAND_PAST, BAND_WIDTH), F32),
        ],
        compiler_params=_cparams(("arbitrary",)),
        name="band_proj",
    )(x, w, wvt)


def _attn_out(acc):
    return (acc[:V_DIM, :] / acc[V_DIM:V_DIM + 1, :]).T


def _mla_attn_kernel(qt_ref, jt_ref, q_ref, k_ref, vt_ref, o_ref, m_ref, acc_ref, sa_ref, sb_ref, ma_ref, mb_ref,
                     *, tq, nblk):
    kc = lax.broadcasted_iota(jnp.int32, (tq, tq), 0) // CHUNK
    qc = lax.broadcasted_iota(jnp.int32, (tq, tq), 1) // CHUNK
    causal = kc <= qc

    def scores(t):
        t = jnp.minimum(t, nblk - 1)
        qo = pl.multiple_of(qt_ref[t] * tq, tq)
        ko = pl.multiple_of(jt_ref[t] * tq, tq)
        return _dot_nt(k_ref[pl.ds(ko, tq), :], q_ref[pl.ds(qo, tq), :])

    def produce(t, buf):
        s_ref, bm_ref = buf
        st = scores(t)
        s_ref[...] = st
        bm_ref[...] = jnp.max(st, axis=0, keepdims=True)

    def fold(j, st, bmax):
        m = m_ref[...]
        m_new = jnp.maximum(m, bmax)
        p = jnp.exp2(st - m_new)
        acc = jnp.exp2(m - m_new) * acc_ref[...] + _dot(vt_ref[j], p.astype(BF16))
        m_ref[...] = m_new
        return acc

    def step(t, cur, nxt):
        qi = qt_ref[t]
        j = jt_ref[t]

        @pl.when(j < qi)
        def _():
            produce(t + 1, nxt)
            acc_ref[...] = fold(j, cur[0][...], cur[1][...])

        @pl.when(j == qi)
        def _():
            produce(t + 1, nxt)
            st = jnp.where(causal, cur[0][...], NEG)
            acc = fold(j, st, jnp.max(st, axis=0, keepdims=True))
            qo = pl.multiple_of(qi * tq, tq)
            o_ref[pl.ds(qo, tq), :] = _attn_out(acc).astype(o_ref.dtype)
            m_ref[...] = jnp.full(m_ref.shape, NEG, F32)
            acc_ref[...] = jnp.zeros(acc_ref.shape, F32)

    m_ref[...] = jnp.full(m_ref.shape, NEG, F32)
    acc_ref[...] = jnp.zeros(acc_ref.shape, F32)
    buf_a = (sa_ref, ma_ref)
    buf_b = (sb_ref, mb_ref)
    produce(0, buf_a)

    def pair(i, c):
        step(2 * i, buf_a, buf_b)
        step(2 * i + 1, buf_b, buf_a)
        return c

    lax.fori_loop(0, nblk // 2, pair, 0)
    if nblk % 2:
        step(nblk - 1, buf_a, buf_b)


def _mla_attn(q, k, vt, seq, tq):
    n = q.shape[0]
    batch = n // seq
    nq = seq // tq
    pairs = [(i, j) for i in range(nq) for j in range(i + 1)]
    qt = jnp.asarray([p[0] for p in pairs], jnp.int32)
    jt = jnp.asarray([p[1] for p in pairs], jnp.int32)
    blk = lambda b, h, qt, jt: (b, h)
    return pl.pallas_call(
        functools.partial(_mla_attn_kernel, tq=tq, nblk=len(pairs)),
        grid_spec=pltpu.PrefetchScalarGridSpec(
            num_scalar_prefetch=2,
            grid=(batch, H_MLA),
            in_specs=[
                pl.BlockSpec((seq, QK_PAD), blk),
                pl.BlockSpec((seq, QK_PAD), blk),
                pl.BlockSpec((nq, VT_ROWS, tq), lambda b, h, qt, jt: (b, h, 0)),
            ],
            out_specs=pl.BlockSpec((seq, V_DIM), blk),
            scratch_shapes=[pltpu.VMEM((1, tq), F32), pltpu.VMEM((VT_ROWS, tq), F32),
                            pltpu.VMEM((tq, tq), F32), pltpu.VMEM((tq, tq), F32),
                            pltpu.VMEM((1, tq), F32), pltpu.VMEM((1, tq), F32)],
        ),
        out_shape=jax.ShapeDtypeStruct((n, MLA_WIDTH), BF16),
        compiler_params=_cparams(("parallel", "parallel")),
        name="mla_attn",
    )(qt, jt, q, k, vt)


def _band_attn_kernel(q_ref, k_ref, vt_ref, b_ref, o_ref, *, nq):
    tq = BAND_PAST
    for i in range(nq):
        q = q_ref[i * tq:(i + 1) * tq, :]
        sc = _dot_nt(k_ref[i * tq:(i + 1) * tq, :], q) + b_ref[tq:, :]
        m = jnp.max(sc, axis=0, keepdims=True)
        if i > 0:
            sp = _dot_nt(k_ref[(i - 1) * tq:i * tq, :], q) + b_ref[:tq, :]
            m = jnp.maximum(m, jnp.max(sp, axis=0, keepdims=True))
            acc = _dot(vt_ref[i - 1], jnp.exp2(sp - m).astype(BF16))
            acc = acc + _dot(vt_ref[i], jnp.exp2(sc - m).astype(BF16))
        else:
            acc = _dot(vt_ref[i], jnp.exp2(sc - m).astype(BF16))
        o_ref[i * tq:(i + 1) * tq, :] = _attn_out(acc).astype(o_ref.dtype)


def _band_attn(q, k, vt, bias_t, seq):
    n = q.shape[0]
    batch = n // seq
    nq = seq // BAND_PAST
    blk = lambda b, h: (b, h)
    return pl.pallas_call(
        functools.partial(_band_attn_kernel, nq=nq),
        grid=(batch, H_BAND),
        in_specs=[
            pl.BlockSpec((seq, D_BAND), blk),
            pl.BlockSpec((seq, D_BAND), blk),
            pl.BlockSpec((nq, VT_ROWS, BAND_PAST), lambda b, h: (b, h, 0)),
            pl.BlockSpec((None, 2 * BAND_PAST, BAND_PAST), lambda b, h: (h, 0, 0)),
        ],
        out_specs=pl.BlockSpec((seq, D_BAND), blk),
        out_shape=jax.ShapeDtypeStruct((n, BAND_WIDTH), BF16),
        compiler_params=_cparams(("parallel", "parallel")),
        name="band_attn",
    )(q, k, vt, bias_t)


def _top2_of4(v, sv):
    best, bi, bs = v[0], jnp.zeros(v[0].shape, jnp.int32), sv[0]
    for i in range(1, GROUP_SIZE):
        c = v[i] > best
        best = jnp.where(c, v[i], best)
        bi = jnp.where(c, i, bi)
        bs = jnp.where(c, sv[i], bs)
    sec = jnp.full(v[0].shape, -jnp.inf, F32)
    si = jnp.zeros(v[0].shape, jnp.int32)
    ss = jnp.zeros(v[0].shape, F32)
    for i in range(GROUP_SIZE):
        c = (bi != i) & (v[i] > sec)
        sec = jnp.where(c, v[i], sec)
        si = jnp.where(c, i, si)
        ss = jnp.where(c, sv[i], ss)
    return best + sec, bi, si, bs, ss


def _route(sel, s):
    groups = []
    for g in range(N_GROUPS):
        rows = range(g * GROUP_SIZE, (g + 1) * GROUP_SIZE)
        groups.append(_top2_of4([sel[i:i + 1, :] for i in rows], [s[i:i + 1, :] for i in rows]))
    score, bi, si, bs, ss = groups[0]
    gi = jnp.zeros(score.shape, jnp.int32)
    for g in range(1, N_GROUPS):
        c = groups[g][0] > score
        score = jnp.where(c, groups[g][0], score)
        gi = jnp.where(c, g, gi)
        bi = jnp.where(c, groups[g][1], bi)
        si = jnp.where(c, groups[g][2], si)
        bs = jnp.where(c, groups[g][3], bs)
        ss = jnp.where(c, groups[g][4], ss)
    den = bs + ss
    return gi * GROUP_SIZE + bi, gi * GROUP_SIZE + si, bs / den, ss / den


def _out_route_kernel(oa_ref, ob_ref, ga_ref, gb_ref, wo_ref, x_ref, lg_ref, lb_ref, wr_ref, br_ref, u_ref,
                      x1_ref, e_ref, r_ref, g_ref, cnt_ref, base_ref, *, alpha):
    @pl.when(pl.program_id(0) == 0)
    def _():
        base_ref[...] = jnp.zeros_like(base_ref)

    ts = u_ref.shape[0]
    base = base_ref[...]
    for sub in range(x_ref.shape[0] // ts):
        rows = slice(sub * ts, (sub + 1) * ts)
        na = _rms(oa_ref[rows, :].astype(F32), ga_ref[...]).astype(BF16)
        nb = _rms(ob_ref[rows, :].astype(F32), gb_ref[...]).astype(BF16)
        mix = _dot(na, wo_ref[:MLA_WIDTH, :]) + _dot(nb, wo_ref[MLA_WIDTH:, :])
        x1 = _layer_norm(alpha * x_ref[rows, :] + mix, lg_ref[...], lb_ref[...])
        x1_ref[rows, :] = x1
        logits = _dot_nt(wr_ref[...], x1.astype(BF16))
        s = 1.0 / (1.0 + jnp.exp(-logits))
        e0, e1, g0, g1 = _route(s + br_ref[...], s)
        e_ref[0:1, rows] = e0
        e_ref[1:2, rows] = e1
        g_ref[0:1, rows] = g0
        g_ref[1:2, rows] = g1
        eid = lax.broadcasted_iota(jnp.int32, logits.shape, 0)
        oh0 = eid == e0
        oh1 = eid == e1
        both = jnp.where(oh0 | oh1, 1.0, 0.0)
        before = _dot(both.astype(BF16), u_ref[...]) + base
        r_ref[0:1, rows] = jnp.sum(jnp.where(oh0, before, 0.0), axis=0, keepdims=True).astype(jnp.int32)
        r_ref[1:2, rows] = jnp.sum(jnp.where(oh1, before, 0.0), axis=0, keepdims=True).astype(jnp.int32)
        base = base + jnp.sum(both, axis=1, keepdims=True)
    base_ref[...] = base
    cnt_ref[...] = base


def _out_route(oa, ob, ga, gb, wo, x, lg, lb, wr, br, alpha, tm, ts):
    n = x.shape[0]
    u = (lax.broadcasted_iota(jnp.int32, (ts, ts), 0) < lax.broadcasted_iota(jnp.int32, (ts, ts), 1)).astype(BF16)
    const = lambda i: (0, 0)
    row = lambda i: (i, 0)
    col = lambda i: (0, i)
    return pl.pallas_call(
        functools.partial(_out_route_kernel, alpha=alpha),
        grid=(n // tm,),
        in_specs=[
            pl.BlockSpec((tm, MLA_WIDTH), row),
            pl.BlockSpec((tm, BAND_WIDTH), row),
            pl.BlockSpec(ga.shape, const),
            pl.BlockSpec(gb.shape, const),
            pl.BlockSpec(wo.shape, const),
            pl.BlockSpec((tm, D_MODEL), row),
            pl.BlockSpec(lg.shape, const),
            pl.BlockSpec(lb.shape, const),
            pl.BlockSpec(wr.shape, const),
            pl.BlockSpec(br.shape, const),
            pl.BlockSpec((ts, ts), const),
        ],
        out_specs=[
            pl.BlockSpec((tm, D_MODEL), row),
            pl.BlockSpec((2, tm), col),
            pl.BlockSpec((2, tm), col),
            pl.BlockSpec((2, tm), col),
            pl.BlockSpec((N_EXPERTS, 1), const),
        ],
        out_shape=[
            jax.ShapeDtypeStruct((n, D_MODEL), F32),
            jax.ShapeDtypeStruct((2, n), jnp.int32),
            jax.ShapeDtypeStruct((2, n), jnp.int32),
            jax.ShapeDtypeStruct((2, n), F32),
            jax.ShapeDtypeStruct((N_EXPERTS, 1), F32),
        ],
        scratch_shapes=[pltpu.VMEM((N_EXPERTS, 1), F32)],
        compiler_params=_cparams(("arbitrary",)),
        name="out_route",
    )(oa, ob, ga, gb, wo, x, lg, lb, wr, br, u)


def _slots_kernel(cnt_ref, e_ref, r_ref, d_ref, blk_ref, ex_ref, lo_ref, hi_ref, *, tb, nblocks):
    shift = tb.bit_length() - 1
    e = e_ref[...]
    dest = r_ref[...]
    item = lax.broadcasted_iota(jnp.int32, blk_ref.shape, 1)
    blk = jnp.zeros(blk_ref.shape, jnp.int32)
    exv = jnp.zeros(blk_ref.shape, jnp.int32)
    lo = jnp.zeros(blk_ref.shape, jnp.int32)
    hi = jnp.zeros(blk_ref.shape, jnp.int32)
    start = jnp.int32(0)
    first = jnp.int32(0)
    last_e = jnp.int32(0)
    for ex in range(N_EXPERTS):
        cnt = cnt_ref[ex]
        end = start + cnt
        sb = lax.shift_right_logical(start, shift)
        nitems = jnp.where(cnt > 0, lax.shift_right_logical(end - 1, shift) - sb + 1, 0)
        dest = dest + jnp.where(e == ex, start, 0)
        mine = (item >= first) & (item < first + nitems)
        b = sb + item - first
        blk = jnp.where(mine, b, blk)
        exv = jnp.where(mine, ex, exv)
        lo = jnp.where(mine, jnp.maximum(start, b * tb) - b * tb, lo)
        hi = jnp.where(mine, jnp.minimum(end, (b + 1) * tb) - b * tb, hi)
        last_e = jnp.where(cnt > 0, ex, last_e)
        start = end
        first = first + nitems
    unused = item >= first
    d_ref[...] = dest
    blk_ref[...] = jnp.where(unused, nblocks - 1, blk)
    ex_ref[...] = jnp.where(unused, last_e, exv)
    lo_ref[...] = lo
    hi_ref[...] = hi


def _slots(cnt, e, r, tb, nwp):
    n = e.shape[1]
    full = lambda shape: pl.BlockSpec(shape, lambda: (0, 0))
    return pl.pallas_call(
        functools.partial(_slots_kernel, tb=tb, nblocks=2 * n // tb),
        in_specs=[pl.BlockSpec(memory_space=pltpu.SMEM), full((2, n)), full((2, n))],
        out_specs=[full((2, n))] + [full((1, nwp))] * 4,
        out_shape=[jax.ShapeDtypeStruct((2, n), jnp.int32)] + [jax.ShapeDtypeStruct((1, nwp), jnp.int32)] * 4,
        name="moe_slots",
    )(cnt, e, r)


def _dispatch_kernel(d_ref, x_ref, xs_ref, sem, *, tm):
    def issue(r, c):
        for k in range(2):
            pltpu.make_async_copy(x_ref.at[pl.ds(r, 1)], xs_ref.at[pl.ds(d_ref[0, 0, k * tm + r], 1)],
                                  sem.at[k]).start(priority=k)
        return c

    lax.fori_loop(0, tm, issue, 0, unroll=ROW_UNROLL)
    for k in range(2):
        pltpu.make_async_copy(x_ref, xs_ref.at[pl.ds(0, tm)], sem.at[k]).wait()


def _dispatch(dest3, x1, nslots, tm):
    n = x1.shape[0]
    return pl.pallas_call(
        functools.partial(_dispatch_kernel, tm=tm),
        grid=(n // tm,),
        in_specs=[
            pl.BlockSpec((1, 1, 2 * tm), lambda i: (i, 0, 0), memory_space=pltpu.SMEM),
            pl.BlockSpec((tm, D_MODEL), lambda i: (i, 0)),
        ],
        out_specs=pl.BlockSpec(memory_space=pl.ANY),
        out_shape=jax.ShapeDtypeStruct((nslots, D_MODEL), F32),
        scratch_shapes=[pltpu.SemaphoreType.DMA((2,))],
        compiler_params=_cparams(("arbitrary",)),
        name="moe_dispatch",
    )(dest3, x1)


def _experts_kernel(blk_ref, ex_ref, lo_ref, hi_ref, xs_ref, wg_ref, wu_ref, wd_ref, y_ref):
    i = pl.program_id(0)
    lo = lo_ref[i]
    hi = hi_ref[i]
    opens_block = (i == 0) | (blk_ref[i] != blk_ref[jnp.maximum(i - 1, 0)])

    @pl.when(lo < hi)
    def _():
        xb = xs_ref[...].astype(BF16)
        g = _dot(xb, wg_ref[...])
        u = _dot(xb, wu_ref[...])
        hid = (g / (1.0 + jnp.exp(-g))) * u
        y = _dot(hid.astype(BF16), wd_ref[...])
        row = lax.broadcasted_iota(jnp.int32, (y.shape[0], 1), 0)
        y = jnp.where((row >= lo) & (row < hi), y, 0.0)

        @pl.when(opens_block)
        def _():
            y_ref[...] = y

        @pl.when(jnp.logical_not(opens_block))
        def _():
            y_ref[...] += y


def _experts(blk, ex, lo, hi, xs, wg, wu, wd, layer, tb, nwork):
    nslots = xs.shape[0]
    rows = lambda i, blk, ex, lo, hi: (blk[i], 0)
    wmap = lambda i, blk, ex, lo, hi: (layer, ex[i], 0, 0)
    return pl.pallas_call(
        _experts_kernel,
        grid_spec=pltpu.PrefetchScalarGridSpec(
            num_scalar_prefetch=4,
            grid=(nwork,),
            in_specs=[
                pl.BlockSpec((tb, D_MODEL), rows),
                pl.BlockSpec((None, None, D_MODEL, EXPERT_FF), wmap),
                pl.BlockSpec((None, None, D_MODEL, EXPERT_FF), wmap),
                pl.BlockSpec((None, None, EXPERT_FF, D_MODEL), wmap),
            ],
            out_specs=pl.BlockSpec((tb, D_MODEL), rows),
        ),
        out_shape=jax.ShapeDtypeStruct((nslots, D_MODEL), F32),
        compiler_params=_cparams(("arbitrary",)),
        name="moe_experts",
    )(blk, ex, lo, hi, xs, wg, wu, wd)


def _combine_kernel(dc_ref, dn_ref, x_ref, gt_ref, lg_ref, lb_ref, y_ref, o_ref, buf, sem, *, tm, alpha):
    i = pl.program_id(0)
    more = i + 1 < pl.num_programs(0)

    def gather(d_ref, s):
        def issue(r, c):
            for k in range(2):
                pltpu.make_async_copy(y_ref.at[pl.ds(d_ref[0, 0, k * tm + r], 1)], buf.at[2 * s + k, pl.ds(r, 1)],
                                      sem.at[s, k]).start(priority=k)
            return c

        lax.fori_loop(0, tm, issue, 0, unroll=ROW_UNROLL)

    @pl.when(i == 0)
    def _():
        gather(dc_ref, 0)

    for s in range(2):
        @pl.when(i % 2 == s)
        def _():
            @pl.when(more)
            def _():
                gather(dn_ref, 1 - s)

            for k in range(2):
                pltpu.make_async_copy(y_ref.at[pl.ds(0, tm)], buf.at[2 * s + k], sem.at[s, k]).wait()
            ffn = buf[2 * s] * gt_ref[:, 0:1] + buf[2 * s + 1] * gt_ref[:, 1:2]
            o_ref[...] = _layer_norm(alpha * x_ref[...] + ffn, lg_ref[...], lb_ref[...])


def _combine(dest3, x1, gates_t, lg, lb, yb, alpha, tm):
    n = x1.shape[0]
    nt = n // tm
    const = lambda i: (0, 0)
    row = lambda i: (i, 0)
    return pl.pallas_call(
        functools.partial(_combine_kernel, tm=tm, alpha=alpha),
        grid=(nt,),
        in_specs=[
            pl.BlockSpec((1, 1, 2 * tm), lambda i: (i, 0, 0), memory_space=pltpu.SMEM),
            pl.BlockSpec((1, 1, 2 * tm), lambda i: (jnp.minimum(i + 1, nt - 1), 0, 0), memory_space=pltpu.SMEM),
            pl.BlockSpec((tm, D_MODEL), row),
            pl.BlockSpec((tm, 2), row),
            pl.BlockSpec(lg.shape, const),
            pl.BlockSpec(lb.shape, const),
            pl.BlockSpec(memory_space=pl.ANY),
        ],
        out_specs=pl.BlockSpec((tm, D_MODEL), row),
        out_shape=jax.ShapeDtypeStruct((n, D_MODEL), F32),
        scratch_shapes=[pltpu.VMEM((4, tm, D_MODEL), F32), pltpu.SemaphoreType.DMA((2, 2))],
        compiler_params=_cparams(("arbitrary",)),
        name="moe_combine",
    )(dest3, dest3, x1, gates_t, lg, lb, yb)


def _moe_ln(x1, e, r, gates, cnt, wg, wu, wd, layer, lg, lb, alpha, tm, tb):
    n = x1.shape[0]
    nwork = (2 * n) // tb + N_EXPERTS - 1
    nwp = -(-nwork // LANE) * LANE
    dest, blk, ex, lo, hi = _slots(cnt.astype(jnp.int32).reshape(N_EXPERTS), e, r, tb, nwp)
    dest3 = dest.reshape(2, n // tm, tm).transpose(1, 0, 2).reshape(n // tm, 1, 2 * tm)
    xs = _dispatch(dest3, x1, 2 * n, tm)
    yb = _experts(blk.reshape(nwp), ex.reshape(nwp), lo.reshape(nwp), hi.reshape(nwp), xs, wg, wu, wd,
                  layer, tb, nwork)
    return _combine(dest3, x1, gates.T, lg, lb, yb, alpha, tm)


def _sample_proj_kernel(x_ref, w_ref, gq_ref, wq_ref, gkv_ref, c_ref, s_ref,
                        q_ref, lat_ref, kpe_ref, qb_ref, kb_ref, vb_ref, acc_ref):
    k = pl.program_id(0)

    @pl.when(k == 0)
    def _():
        acc_ref[...] = jnp.zeros_like(acc_ref)

    acc_ref[...] += _dot(x_ref[...].astype(BF16), w_ref[...].astype(BF16))

    @pl.when(k == pl.num_programs(0) - 1)
    def _():
        cos = c_ref[...]
        sin = s_ref[...]
        cqn = _rms(acc_ref[:, :Q_LORA], gq_ref[...])
        q = _dot(cqn.astype(BF16), wq_ref[...])
        q_ref[:, :H_MLA * NOPE_DIM] = q[:, :H_MLA * NOPE_DIM]
        qr = q[:, H_MLA * NOPE_DIM:]
        q_ref[:, H_MLA * NOPE_DIM:] = qr * cos + _swap32(qr) * sin
        lat_ref[...] = _rms(acc_ref[:, Q_LORA:Q_LORA + KV_LORA], gkv_ref[...])
        a = Q_LORA + KV_LORA
        kv = acc_ref[:, a:a + LANE]
        kpe_ref[...] = (kv * cos[:, :LANE] + _swap32(kv) * sin[:, :LANE])[:, :ROPE_DIM]
        a += ROPE_DIM
        qb_ref[...] = acc_ref[:, a:a + BAND_WIDTH]
        kb_ref[...] = acc_ref[:, a + BAND_WIDTH:a + 2 * BAND_WIDTH]
        vb_ref[...] = acc_ref[:, a + 2 * BAND_WIDTH:a + 3 * BAND_WIDTH]


def _sample_proj(x, w_in, layer, gq, wq, gkv, cos, sin, tk):
    m = x.shape[0]
    width = w_in.shape[2]
    const = lambda k: (0, 0)
    shapes = [(m, wq.shape[1]), (m, KV_LORA), (m, ROPE_DIM), (m, BAND_WIDTH), (m, BAND_WIDTH), (m, BAND_WIDTH)]
    return pl.pallas_call(
        _sample_proj_kernel,
        grid=(D_MODEL // tk,),
        in_specs=[
            pl.BlockSpec((m, tk), lambda k: (0, k)),
            pl.BlockSpec((None, tk, width), lambda k: (layer, k, 0)),
            pl.BlockSpec(gq.shape, const),
            pl.BlockSpec(wq.shape, const),
            pl.BlockSpec(gkv.shape, const),
            pl.BlockSpec(cos.shape, const),
            pl.BlockSpec(sin.shape, const),
        ],
        out_specs=[pl.BlockSpec(s, const) for s in shapes],
        out_shape=[jax.ShapeDtypeStruct(s, F32) for s in shapes],
        scratch_shapes=[pltpu.VMEM((m, width), F32)],
        compiler_params=_cparams(("arbitrary",)),
        name="sample_proj",
    )(x, w_in, gq, wq, gkv, cos, sin)


def _softmax2(sc, sn):
    m = jnp.maximum(jnp.max(sc, axis=-1, keepdims=True), jnp.max(sn, axis=-1, keepdims=True))
    ec = jnp.exp(sc - m)
    en = jnp.exp(sn - m)
    l = jnp.sum(ec, axis=-1, keepdims=True) + jnp.sum(en, axis=-1, keepdims=True)
    return (ec / l).astype(BF16), (en / l).astype(BF16)


def _sample_mla_kernel(q_ref, lat_ref, kpe_ref, cl_ref, cr_ref, wk_ref, wv_ref, o_ref, *, scale):
    lat_c = cl_ref[...].astype(BF16)
    lat_n = lat_ref[...].astype(BF16)
    kpe_c = cr_ref[...].astype(BF16)
    kpe_n = kpe_ref[...].astype(BF16)
    kc = _dot(lat_c, wk_ref[...]).astype(BF16)
    vc = _dot(lat_c, wv_ref[...]).astype(BF16)
    kn = _dot(lat_n, wk_ref[...]).astype(BF16)
    vn = _dot(lat_n, wv_ref[...]).astype(BF16)
    for hd in range(H_MLA):
        a = hd * NOPE_DIM
        b = H_MLA * NOPE_DIM + hd * ROPE_DIM
        qn = q_ref[:, a:a + NOPE_DIM].astype(BF16)
        qr = q_ref[:, b:b + ROPE_DIM].astype(BF16)
        sc = (_dot_nt(qn, kc[:, a:a + NOPE_DIM]) + _dot_nt(qr, kpe_c)) * scale
        sn = (_dot_nt(qn, kn[:, a:a + NOPE_DIM]) + _dot_nt(qr, kpe_n)) * scale
        pc, pn = _softmax2(sc, sn)
        o_ref[:, a:a + V_DIM] = _dot(pc, vc[:, a:a + V_DIM]) + _dot(pn, vn[:, a:a + V_DIM])


def _sample_mla(q, lat, kpe, cache_lat, cache_rope, wk, wv, layer, t):
    m = q.shape[0]
    batch = m // t
    past = cache_lat.shape[2]
    row = lambda b: (b, 0)
    const = lambda b: (0, 0)
    return pl.pallas_call(
        functools.partial(_sample_mla_kernel, scale=(NOPE_DIM + ROPE_DIM) ** -0.5),
        grid=(batch,),
        in_specs=[
            pl.BlockSpec((t, q.shape[1]), row),
            pl.BlockSpec((t, KV_LORA), row),
            pl.BlockSpec((t, ROPE_DIM), row),
            pl.BlockSpec((None, None, past, KV_LORA), lambda b: (layer, b, 0, 0)),
            pl.BlockSpec((None, None, past, ROPE_DIM), lambda b: (layer, b, 0, 0)),
            pl.BlockSpec((None, KV_LORA, MLA_WIDTH), lambda b: (layer, 0, 0)),
            pl.BlockSpec((None, KV_LORA, MLA_WIDTH), lambda b: (layer, 0, 0)),
        ],
        out_specs=pl.BlockSpec((t, MLA_WIDTH), row),
        out_shape=jax.ShapeDtypeStruct((m, MLA_WIDTH), F32),
        compiler_params=_cparams(("parallel",)),
        name="sample_mla",
    )(q, lat, kpe, cache_lat, cache_rope, wk, wv)


def _sample_band_kernel(q_ref, kn_ref, vn_ref, ck_ref, cv_ref, bc_ref, bn_ref, o_ref, *, scale):
    for hd in range(H_BAND):
        a = hd * D_BAND
        q = q_ref[:, a:a + D_BAND].astype(BF16)
        sc = _dot_nt(q, ck_ref[:, a:a + D_BAND].astype(BF16)) * scale + bc_ref[hd]
        sn = _dot_nt(q, kn_ref[:, a:a + D_BAND].astype(BF16)) * scale + bn_ref[hd]
        pc, pn = _softmax2(sc, sn)
        o_ref[:, a:a + D_BAND] = (_dot(pc, cv_ref[:, a:a + D_BAND].astype(BF16))
                                  + _dot(pn, vn_ref[:, a:a + D_BAND].astype(BF16)))


def _sample_band(q, kn, vn, cache_k, cache_v, bias_c, bias_n, layer, t):
    m = q.shape[0]
    batch = m // t
    r = cache_k.shape[2]
    row = lambda b: (b, 0)
    cache = lambda b: (layer, b, 0, 0)
    return pl.pallas_call(
        functools.partial(_sample_band_kernel, scale=D_BAND ** -0.5),
        grid=(batch,),
        in_specs=[
            pl.BlockSpec((t, BAND_WIDTH), row),
            pl.BlockSpec((t, BAND_WIDTH), row),
            pl.BlockSpec((t, BAND_WIDTH), row),
            pl.BlockSpec((None, None, r, BAND_WIDTH), cache),
            pl.BlockSpec((None, None, r, BAND_WIDTH), cache),
            pl.BlockSpec(bias_c.shape, lambda b: (0, 0, 0)),
            pl.BlockSpec(bias_n.shape, lambda b: (0, 0, 0)),
        ],
        out_specs=pl.BlockSpec((t, BAND_WIDTH), row),
        out_shape=jax.ShapeDtypeStruct((m, BAND_WIDTH), F32),
        compiler_params=_cparams(("parallel",)),
        name="sample_band",
    )(q, kn, vn, cache_k, cache_v, bias_c, bias_n)


def _spread(w):
    z = jnp.zeros(w.shape[:-1] + (HALF_ROPE,), w.dtype)
    return jnp.concatenate([w[..., :HALF_ROPE], z, w[..., HALF_ROPE:], z], axis=-1)


def _rope_angles(pos):
    inv = 1.0 / (ROPE_THETA ** (jnp.arange(0, ROPE_DIM, 2, dtype=F32) / ROPE_DIM))
    ang = pos.astype(F32)[:, None] * inv[None, :]
    return jnp.cos(ang), jnp.sin(ang)


def _band_bias_kernel(line_ref, o_ref, *, q0, k0, keys_first):
    nr, nc = o_ref.shape
    table = pltpu.roll(jnp.broadcast_to(line_ref[...], (nr, line_ref.shape[1])), 0, 1, stride=1, stride_axis=0)
    row = lax.broadcasted_iota(jnp.int32, (nr, nc), 0)
    col = lax.broadcasted_iota(jnp.int32, (nr, nc), 1)
    shift = CHUNK * (1 << 20)
    kc = ((row if keys_first else col) + (k0 + shift)) // CHUNK
    qc = ((col if keys_first else row) + (q0 + shift)) // CHUNK
    o_ref[...] = jnp.where((kc <= qc) & (kc >= qc - LEFT_CHUNKS), table[:, :nc], NEG)


def _band_bias_table(rel_bias, q0, nq, k0, nk, keys_first):
    heads = rel_bias.shape[0]
    nr, nc = (nk, nq) if keys_first else (nq, nk)
    period = -(-(nr + nc) // LANE) * LANE
    delta = np.arange(period)
    delta = np.where(delta < nc, delta, delta - period)
    qk = (q0 - k0) + (delta if keys_first else -delta)
    line = rel_bias[:, np.clip(qk, -REL_CLIP, REL_CLIP) + REL_CLIP].reshape(heads, 1, period)
    return pl.pallas_call(
        functools.partial(_band_bias_kernel, q0=q0, k0=k0, keys_first=keys_first),
        grid=(heads,),
        in_specs=[pl.BlockSpec((None, 1, period), lambda h: (h, 0, 0))],
        out_specs=pl.BlockSpec((None, nr, nc), lambda h: (h, 0, 0)),
        out_shape=jax.ShapeDtypeStruct((heads, nr, nc), F32),
        compiler_params=_cparams(("parallel",)),
        name="band_bias",
    )(line)


def kernel(x_prompt, x_sample, cache_mla_latent, cache_mla_rope, cache_band_k, cache_band_v, w_in, g_q_lat, w_q_up, g_kv_lat, w_k_up, w_v_up, rel_bias, g_out_mla, g_out_band, w_out, ln1_g, ln1_b, w_router, b_router, w_gate, w_up, w_down, ln2_g, ln2_b):
    depth = w_in.shape[0]
    batch, seq, _ = x_prompt.shape
    dec_batch, dec_seq, _ = x_sample.shape
    past = cache_mla_latent.shape[2]
    band_r = cache_band_k.shape[2]
    alpha = float((2 * depth) ** 0.25)
    n = batch * seq
    m = dec_batch * dec_seq
    mla_scale = (NOPE_DIM + ROPE_DIM) ** -0.5 * LOG2E
    band_scale = D_BAND ** -0.5 * LOG2E
    a0 = Q_LORA + KV_LORA
    a1 = a0 + ROPE_DIM

    w_a = jnp.concatenate([w_in[:, :, :a0], _spread(w_in[:, :, a0:a1])], axis=-1).astype(BF16)
    w_band = jnp.concatenate([w_in[:, :, a1:a1 + BAND_WIDTH] * band_scale, w_in[:, :, a1 + BAND_WIDTH:]],
                             axis=-1).astype(BF16)
    w_band_vt = w_in[:, :, a1 + 2 * BAND_WIDTH:].transpose(0, 2, 1).astype(BF16)
    wq4 = w_q_up.reshape(depth, Q_LORA, H_MLA, NOPE_DIM + ROPE_DIM)
    wq_pad = jnp.concatenate([wq4[..., :NOPE_DIM], _spread(wq4[..., NOPE_DIM:])], axis=-1) * mla_scale
    wq_pad = wq_pad.reshape(depth, Q_LORA, H_MLA * QK_PAD).astype(BF16)
    wq_s = jnp.concatenate([wq4[..., :NOPE_DIM].reshape(depth, Q_LORA, -1),
                            wq4[..., NOPE_DIM:].reshape(depth, Q_LORA, -1)], axis=-1).astype(BF16)
    wk4 = w_k_up.reshape(depth, KV_LORA, H_MLA, NOPE_DIM)
    wk_pad = jnp.concatenate([wk4, jnp.zeros_like(wk4)], axis=-1).reshape(depth, KV_LORA, H_MLA * QK_PAD)
    wk_pad = wk_pad.astype(BF16)
    wk_b = w_k_up.astype(BF16)
    wv_b = w_v_up.astype(BF16)
    wv_t = w_v_up.transpose(0, 2, 1).astype(BF16)
    w_out_b = w_out.astype(BF16)
    wg_b = w_gate.astype(BF16)
    wu_b = w_up.astype(BF16)
    wd_b = w_down.astype(BF16)
    wr_tb = w_router.T.astype(BF16)
    br = b_router.reshape(N_EXPERTS, 1)
    row2 = lambda a: a.reshape(depth, 1, -1)
    gq, gkv, ga, gb = row2(g_q_lat), row2(g_kv_lat), row2(g_out_mla), row2(g_out_band)
    l1g, l1b, l2g, l2b = row2(ln1_g), row2(ln1_b), row2(ln2_g), row2(ln2_b)

    pos_p = jnp.arange(seq, dtype=jnp.int32)
    cos_p, sin_p = _rope_angles(pos_p)
    z = jnp.zeros_like(cos_p)
    cos_sp = jnp.concatenate([cos_p, z, cos_p, z], axis=-1)
    sin_sp = jnp.concatenate([-sin_p, z, sin_p, z], axis=-1)
    pos_s = past + jnp.arange(dec_seq, dtype=jnp.int32)
    cos_s, sin_s = _rope_angles(pos_s)
    cos_s = jnp.tile(jnp.concatenate([cos_s, cos_s], axis=-1), (dec_batch, H_MLA))
    sin_s = jnp.tile(jnp.concatenate([-sin_s, sin_s], axis=-1), (dec_batch, H_MLA))

    tm_p = 256
    states = [[], [], [], [], [], [], [], []]
    xp = x_prompt.reshape(n, D_MODEL)
    xs = x_sample.reshape(m, D_MODEL)
    ck = cache_band_k.reshape(depth, dec_batch, band_r, BAND_WIDTH)
    cv = cache_band_v.reshape(depth, dec_batch, band_r, BAND_WIDTH)
    for l in range(depth):
        q, k, vt, lat, kpe = _mla_proj(xp, w_a[l], gq[l], wq_pad[l], gkv[l], wk_pad[l], wv_t[l],
                                       cos_sp, sin_sp, seq, tm_p, BAND_PAST)
        qb, kb, vbt, ks, vs = _band_proj(xp, w_band[l], w_band_vt[l], seq, tm_p)
        oa = _mla_attn(q, k, vt, seq, BAND_PAST)
        bias_p = _band_bias_table(rel_bias[l] * LOG2E, 0, BAND_PAST, -BAND_PAST, 2 * BAND_PAST, True)
        ob = _band_attn(qb, kb, vbt, bias_p, seq)
        x1, e, r, gates, cnt = _out_route(oa, ob, ga[l], gb[l], w_out_b[l], xp, l1g[l], l1b[l], wr_tb, br,
                                          alpha, 2 * tm_p, tm_p)
        xp = _moe_ln(x1, e, r, gates, cnt, wg_b, wu_b, wd_b, l, l2g[l], l2b[l], alpha, tm_p, 256)
        for acc, s in zip(states[:4], (lat, kpe, ks, vs)):
            acc.append(s)
        sq, slat, skpe, sqb, skb, svb = _sample_proj(xs, w_in, l, gq[l], wq_s[l], gkv[l], cos_s, sin_s, 256)
        soa = _sample_mla(sq, slat, skpe, cache_mla_latent, cache_mla_rope, wk_b, wv_b, l, dec_seq)
        bias_c = _band_bias_table(rel_bias[l], past, dec_seq, past - band_r, band_r, False)
        bias_n = _band_bias_table(rel_bias[l], past, dec_seq, past, dec_seq, False)
        sob = _sample_band(sqb, skb, svb, ck, cv, bias_c, bias_n, l, dec_seq)
        sx1, se, sr, sgates, scnt = _out_route(soa, sob, ga[l], gb[l], w_out_b[l], xs, l1g[l], l1b[l], wr_tb, br,
                                               alpha, m, m)
        xs = _moe_ln(sx1, se, sr, sgates, scnt, wg_b, wu_b, wd_b, l, l2g[l], l2b[l], alpha, m, 128)
        for acc, s in zip(states[4:], (slat, skpe, skb, svb)):
            acc.append(s)

    st = [jnp.stack(a) for a in states]
    return (xp.reshape(batch, seq, D_MODEL), xs.reshape(dec_batch, dec_seq, D_MODEL),
            st[0].reshape(depth, batch, seq, KV_LORA), st[1].reshape(depth, batch, seq, ROPE_DIM),
            st[2].reshape(depth, batch, BAND_PAST, H_BAND, D_BAND),
            st[3].reshape(depth, batch, BAND_PAST, H_BAND, D_BAND),
            st[4].reshape(depth, dec_batch, dec_seq, KV_LORA), st[5].reshape(depth, dec_batch, dec_seq, ROPE_DIM),
            st[6].reshape(depth, dec_batch, dec_seq, H_BAND, D_BAND),
            st[7].reshape(depth, dec_batch, dec_seq, H_BAND, D_BAND))
```

```python
import functools

import jax
import jax.numpy as jnp
import numpy as np
from jax import lax
from jax.experimental import pallas as pl
from jax.experimental.pallas import tpu as pltpu

D_MODEL = 2048
CHUNK = 64
H_MLA = 8
NOPE_DIM = 128
ROPE_DIM = 64
HALF_ROPE = ROPE_DIM // 2
V_DIM = 128
Q_LORA = 512
KV_LORA = 256
ROPE_THETA = 10000.0
H_BAND = 8
D_BAND = 128
LEFT_CHUNKS = 8
BAND_PAST = LEFT_CHUNKS * CHUNK
REL_CLIP = 128
MLA_WIDTH = H_MLA * V_DIM
BAND_WIDTH = H_BAND * D_BAND
QK_PAD = 256
N_EXPERTS = 16
N_GROUPS = 4
GROUP_SIZE = 4
EXPERT_FF = 512
RMS_EPS = 1e-6
LN_EPS = 1e-5
NEG = -1e30
LOG2E = 1.4426950408889634
LANE = 128
VMEM_LIMIT = 56 * 1024 * 1024

BF16_ROWS = 16
VT_ROWS = V_DIM + BF16_ROWS
ROW_UNROLL = 8

F32 = jnp.float32
BF16 = jnp.bfloat16


def _cparams(sem):
    return pltpu.CompilerParams(dimension_semantics=sem, vmem_limit_bytes=VMEM_LIMIT)


def _dot(a, b):
    return jnp.dot(a, b, preferred_element_type=F32)


def _dot_nt(a, b):
    return lax.dot_general(a, b, (((1,), (1,)), ((), ())), preferred_element_type=F32)


def _rms(x, g):
    return x * lax.rsqrt(jnp.mean(x * x, axis=-1, keepdims=True) + RMS_EPS) * g


def _layer_norm(x, g, b):
    mu = jnp.mean(x, axis=-1, keepdims=True)
    xc = x - mu
    var = jnp.mean(xc * xc, axis=-1, keepdims=True)
    return xc * lax.rsqrt(var + LN_EPS) * g + b


def _swap32(x):
    w = x.shape[-1]
    lane = lax.broadcasted_iota(jnp.int32, x.shape, x.ndim - 1)
    return jnp.where((lane % ROPE_DIM) < HALF_ROPE, pltpu.roll(x, w - HALF_ROPE, x.ndim - 1),
                     pltpu.roll(x, HALF_ROPE, x.ndim - 1))


def _mla_proj_kernel(x_ref, wa_ref, gq_ref, wq_ref, gkv_ref, wk_ref, wv_ref, c_ref, s_ref,
                     q_ref, k_ref, v_ref, lat_ref, kpe_ref):
    xb = x_ref[...].astype(BF16)
    h = _dot(xb, wa_ref[...])
    cqn = _rms(h[:, :Q_LORA], gq_ref[...])
    lat = _rms(h[:, Q_LORA:Q_LORA + KV_LORA], gkv_ref[...])
    kp = h[:, Q_LORA + KV_LORA:]
    cos = c_ref[...]
    sin = s_ref[...]
    kpr = kp * cos + pltpu.roll(kp, 2 * HALF_ROPE, 1) * sin
    lat_ref[...] = lat
    kpe_ref[...] = (kpr + pltpu.roll(kpr, LANE - HALF_ROPE, 1))[:, :ROPE_DIM]
    kprb = kpr.astype(BF16)
    latb = lat.astype(BF16)
    q = _dot(cqn.astype(BF16), wq_ref[...])
    kn = _dot(latb, wk_ref[...])
    vt = _dot_nt(wv_ref[...], latb).astype(BF16)
    ones = jnp.ones((BF16_ROWS, vt.shape[1]), BF16)
    for hd in range(H_MLA):
        v_ref[hd * VT_ROWS:hd * VT_ROWS + V_DIM, :] = vt[hd * V_DIM:(hd + 1) * V_DIM, :]
        v_ref[hd * VT_ROWS + V_DIM:(hd + 1) * VT_ROWS, :] = ones
    for hd in range(H_MLA):
        a = hd * QK_PAD
        qv = q[:, a + NOPE_DIM:a + QK_PAD]
        q_ref[:, a:a + NOPE_DIM] = q[:, a:a + NOPE_DIM].astype(BF16)
        q_ref[:, a + NOPE_DIM:a + QK_PAD] = (qv * cos + pltpu.roll(qv, 2 * HALF_ROPE, 1) * sin).astype(BF16)
        k_ref[:, a:a + NOPE_DIM] = kn[:, a:a + NOPE_DIM].astype(BF16)
        k_ref[:, a + NOPE_DIM:a + QK_PAD] = kprb


def _mla_proj(x, wa, gq, wq, gkv, wk, wv, cos, sin, seq, tm, tk):
    n = x.shape[0]
    tpb = seq // tm
    per_kv = tk // tm
    const = lambda i: (0, 0)
    row = lambda i: (i, 0)
    return pl.pallas_call(
        _mla_proj_kernel,
        grid=(n // tm,),
        in_specs=[
            pl.BlockSpec((tm, D_MODEL), row),
            pl.BlockSpec(wa.shape, const),
            pl.BlockSpec(gq.shape, const),
            pl.BlockSpec(wq.shape, const),
            pl.BlockSpec(gkv.shape, const),
            pl.BlockSpec(wk.shape, const),
            pl.BlockSpec(wv.shape, const),
            pl.BlockSpec((tm, LANE), lambda i: (i % tpb, 0)),
            pl.BlockSpec((tm, LANE), lambda i: (i % tpb, 0)),
        ],
        out_specs=[
            pl.BlockSpec((tm, H_MLA * QK_PAD), row),
            pl.BlockSpec((tm, H_MLA * QK_PAD), row),
            pl.BlockSpec((None, H_MLA * VT_ROWS, tm), lambda i: (i // per_kv, 0, i % per_kv)),
            pl.BlockSpec((tm, KV_LORA), row),
            pl.BlockSpec((tm, ROPE_DIM), row),
        ],
        out_shape=[
            jax.ShapeDtypeStruct((n, H_MLA * QK_PAD), BF16),
            jax.ShapeDtypeStruct((n, H_MLA * QK_PAD), BF16),
            jax.ShapeDtypeStruct((n // tk, H_MLA * VT_ROWS, tk), BF16),
            jax.ShapeDtypeStruct((n, KV_LORA), F32),
            jax.ShapeDtypeStruct((n, ROPE_DIM), F32),
        ],
        compiler_params=_cparams(("parallel",)),
        name="mla_proj",
    )(x, wa, gq, wq, gkv, wk, wv, cos, sin)


def _band_proj_kernel(x_ref, w_ref, wvt_ref, q_ref, k_ref, v_ref, ks_ref, vs_ref, *, tpb, ntail):
    xb = x_ref[...].astype(BF16)
    q_ref[...] = _dot(xb, w_ref[:, :BAND_WIDTH]).astype(BF16)
    k = _dot(xb, w_ref[:, BAND_WIDTH:2 * BAND_WIDTH])
    k_ref[...] = k.astype(BF16)
    vt = _dot_nt(wvt_ref[...], xb).astype(BF16)
    ones = jnp.ones((BF16_ROWS, vt.shape[1]), BF16)
    for hd in range(H_BAND):
        v_ref[hd * VT_ROWS:hd * VT_ROWS + D_BAND, :] = vt[hd * D_BAND:(hd + 1) * D_BAND, :]
        v_ref[hd * VT_ROWS + D_BAND:(hd + 1) * VT_ROWS, :] = ones

    @pl.when(pl.program_id(0) % tpb >= tpb - ntail)
    def _():
        ks_ref[...] = k
        vs_ref[...] = _dot(xb, w_ref[:, 2 * BAND_WIDTH:])


def _band_proj(x, w, wvt, seq, tm):
    n = x.shape[0]
    tpb = seq // tm
    ntail = BAND_PAST // tm
    batch = n // seq
    const = lambda i: (0, 0)
    row = lambda i: (i, 0)
    tail = lambda i: (i // tpb, jnp.maximum(i % tpb - (tpb - ntail), 0), 0)
    return pl.pallas_call(
        functools.partial(_band_proj_kernel, tpb=tpb, ntail=ntail),
        grid=(n // tm,),
        in_specs=[pl.BlockSpec((tm, D_MODEL), row), pl.BlockSpec(w.shape, const), pl.BlockSpec(wvt.shape, const)],
        out_specs=[
            pl.BlockSpec((tm, BAND_WIDTH), row),
            pl.BlockSpec((tm, BAND_WIDTH), row),
            pl.BlockSpec((None, H_BAND * VT_ROWS, tm), lambda i: (i // ntail, 0, i % ntail)),
            pl.BlockSpec((None, tm, BAND_WIDTH), tail),
            pl.BlockSpec((None, tm, BAND_WIDTH), tail),
        ],
        out_shape=[
            jax.ShapeDtypeStruct((n, BAND_WIDTH), BF16),
            jax.ShapeDtypeStruct((n, BAND_WIDTH), BF16),
            jax.ShapeDtypeStruct((n // BAND_PAST, H_BAND * VT_ROWS, BAND_PAST), BF16),
            jax.ShapeDtypeStruct((batch, BAND_PAST, BAND_WIDTH), F32),
            jax.ShapeDtypeStruct((batch, BAND_PAST, BAND_WIDTH), F32),
        ],
        compiler_params=_cparams(("arbitrary",)),
        name="band_proj",
    )(x, w, wvt)


def _attn_out(acc):
    return (acc[:V_DIM, :] / acc[V_DIM:V_DIM + 1, :]).T


def _mla_attn_kernel(qt_ref, jt_ref, q_ref, k_ref, vt_ref, o_ref, m_ref, acc_ref, sa_ref, sb_ref, ma_ref, mb_ref,
                     *, tk, nblk):
    tq = 2 * tk
    kc = lax.broadcasted_iota(jnp.int32, (tk, tk), 0) // CHUNK
    qc = lax.broadcasted_iota(jnp.int32, (tk, tk), 1) // CHUNK
    causal = kc <= qc

    def produce(t, buf):
        s_ref, bm_ref = buf
        t = jnp.minimum(t, nblk - 1)
        qo = pl.multiple_of(qt_ref[t] * tq, tq)
        ko = pl.multiple_of(jt_ref[t] * tk, tk)
        st = _dot_nt(k_ref[pl.ds(ko, tk), :], q_ref[pl.ds(qo, tq), :])
        s_ref[...] = st
        bm_ref[...] = jnp.max(st, axis=0, keepdims=True)

    def fold(j, st, bmax, lanes):
        m = m_ref[:, lanes]
        m_new = jnp.maximum(m, bmax)
        p = jnp.exp2(st - m_new)
        acc_ref[:, lanes] = jnp.exp2(m - m_new) * acc_ref[:, lanes] + _dot(vt_ref[j], p.astype(BF16))
        m_ref[:, lanes] = m_new

    everything = slice(0, tq)
    lower = slice(0, tk)
    upper = slice(tk, tq)

    def step(t, cur, nxt):
        qi = qt_ref[t]
        j = jt_ref[t]

        @pl.when(j < 2 * qi)
        def _():
            produce(t + 1, nxt)
            fold(j, cur[0][...], cur[1][...], everything)

        @pl.when(j == 2 * qi)
        def _():
            produce(t + 1, nxt)
            st = jnp.where(causal, cur[0][:, lower], NEG)
            fold(j, st, jnp.max(st, axis=0, keepdims=True), lower)
            fold(j, cur[0][:, upper], cur[1][:, upper], upper)

        @pl.when(j == 2 * qi + 1)
        def _():
            produce(t + 1, nxt)
            st = jnp.where(causal, cur[0][:, upper], NEG)
            fold(j, st, jnp.max(st, axis=0, keepdims=True), upper)
            qo = pl.multiple_of(qi * tq, tq)
            o_ref[pl.ds(qo, tq), :] = _attn_out(acc_ref[...]).astype(o_ref.dtype)
            m_ref[...] = jnp.full(m_ref.shape, NEG, F32)
            acc_ref[...] = jnp.zeros(acc_ref.shape, F32)

    m_ref[...] = jnp.full(m_ref.shape, NEG, F32)
    acc_ref[...] = jnp.zeros(acc_ref.shape, F32)
    buf_a = (sa_ref, ma_ref)
    buf_b = (sb_ref, mb_ref)
    produce(0, buf_a)

    def pair(i, c):
        step(2 * i, buf_a, buf_b)
        step(2 * i + 1, buf_b, buf_a)
        return c

    lax.fori_loop(0, nblk // 2, pair, 0)


def _mla_attn(q, k, vt, seq, tk):
    n = q.shape[0]
    batch = n // seq
    tq = 2 * tk
    assert seq % tq == 0
    pairs = [(i, j) for i in range(seq // tq) for j in range(2 * i + 2)]
    qt = jnp.asarray([p[0] for p in pairs], jnp.int32)
    jt = jnp.asarray([p[1] for p in pairs], jnp.int32)
    blk = lambda b, h, qt, jt: (b, h)
    return pl.pallas_call(
        functools.partial(_mla_attn_kernel, tk=tk, nblk=len(pairs)),
        grid_spec=pltpu.PrefetchScalarGridSpec(
            num_scalar_prefetch=2,
            grid=(batch, H_MLA),
            in_specs=[
                pl.BlockSpec((seq, QK_PAD), blk),
                pl.BlockSpec((seq, QK_PAD), blk),
                pl.BlockSpec((seq // tk, VT_ROWS, tk), lambda b, h, qt, jt: (b, h, 0)),
            ],
            out_specs=pl.BlockSpec((seq, V_DIM), blk),
            scratch_shapes=[pltpu.VMEM((1, tq), F32), pltpu.VMEM((VT_ROWS, tq), F32),
                            pltpu.VMEM((tk, tq), F32), pltpu.VMEM((tk, tq), F32),
                            pltpu.VMEM((1, tq), F32), pltpu.VMEM((1, tq), F32)],
        ),
        out_shape=jax.ShapeDtypeStruct((n, MLA_WIDTH), BF16),
        compiler_params=_cparams(("parallel", "parallel")),
        name="mla_attn",
    )(qt, jt, q, k, vt)


def _band_attn_kernel(q_ref, k_ref, vt_ref, b_ref, o_ref, *, nq):
    tq = BAND_PAST
    for i in range(nq):
        q = q_ref[i * tq:(i + 1) * tq, :]
        sc = _dot_nt(k_ref[i * tq:(i + 1) * tq, :], q) + b_ref[tq:, :]
        m = jnp.max(sc, axis=0, keepdims=True)
        if i > 0:
            sp = _dot_nt(k_ref[(i - 1) * tq:i * tq, :], q) + b_ref[:tq, :]
            m = jnp.maximum(m, jnp.max(sp, axis=0, keepdims=True))
            acc = _dot(vt_ref[i - 1], jnp.exp2(sp - m).astype(BF16))
            acc = acc + _dot(vt_ref[i], jnp.exp2(sc - m).astype(BF16))
        else:
            acc = _dot(vt_ref[i], jnp.exp2(sc - m).astype(BF16))
        o_ref[i * tq:(i + 1) * tq, :] = _attn_out(acc).astype(o_ref.dtype)


def _band_attn(q, k, vt, bias_t, seq):
    n = q.shape[0]
    batch = n // seq
    nq = seq // BAND_PAST
    blk = lambda b, h: (b, h)
    return pl.pallas_call(
        functools.partial(_band_attn_kernel, nq=nq),
        grid=(batch, H_BAND),
        in_specs=[
            pl.BlockSpec((seq, D_BAND), blk),
            pl.BlockSpec((seq, D_BAND), blk),
            pl.BlockSpec((nq, VT_ROWS, BAND_PAST), lambda b, h: (b, h, 0)),
            pl.BlockSpec((None, 2 * BAND_PAST, BAND_PAST), lambda b, h: (h, 0, 0)),
        ],
        out_specs=pl.BlockSpec((seq, D_BAND), blk),
        out_shape=jax.ShapeDtypeStruct((n, BAND_WIDTH), BF16),
        compiler_params=_cparams(("parallel", "parallel")),
        name="band_attn",
    )(q, k, vt, bias_t)


def _top2_of4(v, sv):
    best, bi, bs = v[0], jnp.zeros(v[0].shape, jnp.int32), sv[0]
    for i in range(1, GROUP_SIZE):
        c = v[i] > best
        best = jnp.where(c, v[i], best)
        bi = jnp.where(c, i, bi)
        bs = jnp.where(c, sv[i], bs)
    sec = jnp.full(v[0].shape, -jnp.inf, F32)
    si = jnp.zeros(v[0].shape, jnp.int32)
    ss = jnp.zeros(v[0].shape, F32)
    for i in range(GROUP_SIZE):
        c = (bi != i) & (v[i] > sec)
        sec = jnp.where(c, v[i], sec)
        si = jnp.where(c, i, si)
        ss = jnp.where(c, sv[i], ss)
    return best + sec, bi, si, bs, ss


def _route(sel, s):
    groups = []
    for g in range(N_GROUPS):
        rows = range(g * GROUP_SIZE, (g + 1) * GROUP_SIZE)
        groups.append(_top2_of4([sel[i:i + 1, :] for i in rows], [s[i:i + 1, :] for i in rows]))
    score, bi, si, bs, ss = groups[0]
    gi = jnp.zeros(score.shape, jnp.int32)
    for g in range(1, N_GROUPS):
        c = groups[g][0] > score
        score = jnp.where(c, groups[g][0], score)
        gi = jnp.where(c, g, gi)
        bi = jnp.where(c, groups[g][1], bi)
        si = jnp.where(c, groups[g][2], si)
        bs = jnp.where(c, groups[g][3], bs)
        ss = jnp.where(c, groups[g][4], ss)
    den = bs + ss
    return gi * GROUP_SIZE + bi, gi * GROUP_SIZE + si, bs / den, ss / den


def _out_route_kernel(oa_ref, ob_ref, ga_ref, gb_ref, wo_ref, x_ref, lg_ref, lb_ref, wr_ref, br_ref, u_ref,
                      x1_ref, e_ref, r_ref, g_ref, cnt_ref, base_ref, *, alpha):
    @pl.when(pl.program_id(0) == 0)
    def _():
        base_ref[...] = jnp.zeros_like(base_ref)

    ts = u_ref.shape[0]
    base = base_ref[...]
    for sub in range(x_ref.shape[0] // ts):
        rows = slice(sub * ts, (sub + 1) * ts)
        na = _rms(oa_ref[rows, :].astype(F32), ga_ref[...]).astype(BF16)
        nb = _rms(ob_ref[rows, :].astype(F32), gb_ref[...]).astype(BF16)
        mix = _dot(na, wo_ref[:MLA_WIDTH, :]) + _dot(nb, wo_ref[MLA_WIDTH:, :])
        x1 = _layer_norm(alpha * x_ref[rows, :] + mix, lg_ref[...], lb_ref[...])
        x1_ref[rows, :] = x1
        logits = _dot_nt(wr_ref[...], x1.astype(BF16))
        s = 1.0 / (1.0 + jnp.exp(-logits))
        e0, e1, g0, g1 = _route(s + br_ref[...], s)
        e_ref[0:1, rows] = e0
        e_ref[1:2, rows] = e1
        g_ref[0:1, rows] = g0
        g_ref[1:2, rows] = g1
        eid = lax.broadcasted_iota(jnp.int32, logits.shape, 0)
        oh0 = eid == e0
        oh1 = eid == e1
        both = jnp.where(oh0 | oh1, 1.0, 0.0)
        before = _dot(both.astype(BF16), u_ref[...]) + base
        r_ref[0:1, rows] = jnp.sum(jnp.where(oh0, before, 0.0), axis=0, keepdims=True).astype(jnp.int32)
        r_ref[1:2, rows] = jnp.sum(jnp.where(oh1, before, 0.0), axis=0, keepdims=True).astype(jnp.int32)
        base = base + jnp.sum(both, axis=1, keepdims=True)
    base_ref[...] = base
    cnt_ref[...] = base


def _out_route(oa, ob, ga, gb, wo, x, lg, lb, wr, br, alpha, tm, ts):
    n = x.shape[0]
    u = (lax.broadcasted_iota(jnp.int32, (ts, ts), 0) < lax.broadcasted_iota(jnp.int32, (ts, ts), 1)).astype(BF16)
    const = lambda i: (0, 0)
    row = lambda i: (i, 0)
    col = lambda i: (0, i)
    return pl.pallas_call(
        functools.partial(_out_route_kernel, alpha=alpha),
        grid=(n // tm,),
        in_specs=[
            pl.BlockSpec((tm, MLA_WIDTH), row),
            pl.BlockSpec((tm, BAND_WIDTH), row),
            pl.BlockSpec(ga.shape, const),
            pl.BlockSpec(gb.shape, const),
            pl.BlockSpec(wo.shape, const),
            pl.BlockSpec((tm, D_MODEL), row),
            pl.BlockSpec(lg.shape, const),
            pl.BlockSpec(lb.shape, const),
            pl.BlockSpec(wr.shape, const),
            pl.BlockSpec(br.shape, const),
            pl.BlockSpec((ts, ts), const),
        ],
        out_specs=[
            pl.BlockSpec((tm, D_MODEL), row),
            pl.BlockSpec((2, tm), col),
            pl.BlockSpec((2, tm), col),
            pl.BlockSpec((2, tm), col),
            pl.BlockSpec((N_EXPERTS, 1), const),
        ],
        out_shape=[
            jax.ShapeDtypeStruct((n, D_MODEL), F32),
            jax.ShapeDtypeStruct((2, n), jnp.int32),
            jax.ShapeDtypeStruct((2, n), jnp.int32),
            jax.ShapeDtypeStruct((2, n), F32),
            jax.ShapeDtypeStruct((N_EXPERTS, 1), F32),
        ],
        scratch_shapes=[pltpu.VMEM((N_EXPERTS, 1), F32)],
        compiler_params=_cparams(("arbitrary",)),
        name="out_route",
    )(oa, ob, ga, gb, wo, x, lg, lb, wr, br, u)


def _slots_kernel(cnt_ref, e_ref, r_ref, d_ref, blk_ref, ex_ref, lo_ref, hi_ref, *, tb, nblocks):
    shift = tb.bit_length() - 1
    e = e_ref[...]
    dest = r_ref[...]
    item = lax.broadcasted_iota(jnp.int32, blk_ref.shape, 1)
    blk = jnp.zeros(blk_ref.shape, jnp.int32)
    exv = jnp.zeros(blk_ref.shape, jnp.int32)
    lo = jnp.zeros(blk_ref.shape, jnp.int32)
    hi = jnp.zeros(blk_ref.shape, jnp.int32)
    start = jnp.int32(0)
    first = jnp.int32(0)
    last_e = jnp.int32(0)
    for ex in range(N_EXPERTS):
        cnt = cnt_ref[ex]
        end = start + cnt
        sb = lax.shift_right_logical(start, shift)
        nitems = jnp.where(cnt > 0, lax.shift_right_logical(end - 1, shift) - sb + 1, 0)
        dest = dest + jnp.where(e == ex, start, 0)
        mine = (item >= first) & (item < first + nitems)
        b = sb + item - first
        blk = jnp.where(mine, b, blk)
        exv = jnp.where(mine, ex, exv)
        lo = jnp.where(mine, jnp.maximum(start, b * tb) - b * tb, lo)
        hi = jnp.where(mine, jnp.minimum(end, (b + 1) * tb) - b * tb, hi)
        last_e = jnp.where(cnt > 0, ex, last_e)
        start = end
        first = first + nitems
    unused = item >= first
    d_ref[...] = dest
    blk_ref[...] = jnp.where(unused, nblocks - 1, blk)
    ex_ref[...] = jnp.where(unused, last_e, exv)
    lo_ref[...] = lo
    hi_ref[...] = hi


def _slots(cnt, e, r, tb, nwp):
    n = e.shape[1]
    full = lambda shape: pl.BlockSpec(shape, lambda: (0, 0))
    return pl.pallas_call(
        functools.partial(_slots_kernel, tb=tb, nblocks=2 * n // tb),
        in_specs=[pl.BlockSpec(memory_space=pltpu.SMEM), full((2, n)), full((2, n))],
        out_specs=[full((2, n))] + [full((1, nwp))] * 4,
        out_shape=[jax.ShapeDtypeStruct((2, n), jnp.int32)] + [jax.ShapeDtypeStruct((1, nwp), jnp.int32)] * 4,
        name="moe_slots",
    )(cnt, e, r)


def _dispatch_kernel(d_ref, x_ref, xs_ref, sem, *, tm):
    def issue(r, c):
        for k in range(2):
            pltpu.make_async_copy(x_ref.at[pl.ds(r, 1)], xs_ref.at[pl.ds(d_ref[0, 0, k * tm + r], 1)],
                                  sem.at[k]).start(priority=k)
        return c

    lax.fori_loop(0, tm, issue, 0, unroll=ROW_UNROLL)
    for k in range(2):
        pltpu.make_async_copy(x_ref, xs_ref.at[pl.ds(0, tm)], sem.at[k]).wait()


def _dispatch(dest3, x1, nslots, tm):
    n = x1.shape[0]
    return pl.pallas_call(
        functools.partial(_dispatch_kernel, tm=tm),
        grid=(n // tm,),
        in_specs=[
            pl.BlockSpec((1, 1, 2 * tm), lambda i: (i, 0, 0), memory_space=pltpu.SMEM),
            pl.BlockSpec((tm, D_MODEL), lambda i: (i, 0)),
        ],
        out_specs=pl.BlockSpec(memory_space=pl.ANY),
        out_shape=jax.ShapeDtypeStruct((nslots, D_MODEL), F32),
        scratch_shapes=[pltpu.SemaphoreType.DMA((2,))],
        compiler_params=_cparams(("arbitrary",)),
        name="moe_dispatch",
    )(dest3, x1)


def _experts_kernel(blk_ref, ex_ref, lo_ref, hi_ref, xs_ref, wg_ref, wu_ref, wd_ref, y_ref):
    i = pl.program_id(0)
    lo = lo_ref[i]
    hi = hi_ref[i]
    opens_block = (i == 0) | (blk_ref[i] != blk_ref[jnp.maximum(i - 1, 0)])

    @pl.when(lo < hi)
    def _():
        xb = xs_ref[...].astype(BF16)
        g = _dot(xb, wg_ref[...])
        u = _dot(xb, wu_ref[...])
        hid = (g / (1.0 + jnp.exp(-g))) * u
        y = _dot(hid.astype(BF16), wd_ref[...])
        row = lax.broadcasted_iota(jnp.int32, (y.shape[0], 1), 0)
        y = jnp.where((row >= lo) & (row < hi), y, 0.0)

        @pl.when(opens_block)
        def _():
            y_ref[...] = y

        @pl.when(jnp.logical_not(opens_block))
        def _():
            y_ref[...] += y


def _experts(blk, ex, lo, hi, xs, wg, wu, wd, layer, tb, nwork):
    nslots = xs.shape[0]
    rows = lambda i, blk, ex, lo, hi: (blk[i], 0)
    wmap = lambda i, blk, ex, lo, hi: (layer, ex[i], 0, 0)
    return pl.pallas_call(
        _experts_kernel,
        grid_spec=pltpu.PrefetchScalarGridSpec(
            num_scalar_prefetch=4,
            grid=(nwork,),
            in_specs=[
                pl.BlockSpec((tb, D_MODEL), rows),
                pl.BlockSpec((None, None, D_MODEL, EXPERT_FF), wmap),
                pl.BlockSpec((None, None, D_MODEL, EXPERT_FF), wmap),
                pl.BlockSpec((None, None, EXPERT_FF, D_MODEL), wmap),
            ],
            out_specs=pl.BlockSpec((tb, D_MODEL), rows),
        ),
        out_shape=jax.ShapeDtypeStruct((nslots, D_MODEL), F32),
        compiler_params=_cparams(("arbitrary",)),
        name="moe_experts",
    )(blk, ex, lo, hi, xs, wg, wu, wd)


def _combine_kernel(dc_ref, dn_ref, x_ref, gt_ref, lg_ref, lb_ref, y_ref, o_ref, buf, sem, *, tm, alpha):
    i = pl.program_id(0)
    more = i + 1 < pl.num_programs(0)

    def gather(d_ref, s):
        def issue(r, c):
            for k in range(2):
                pltpu.make_async_copy(y_ref.at[pl.ds(d_ref[0, 0, k * tm + r], 1)], buf.at[2 * s + k, pl.ds(r, 1)],
                                      sem.at[s, k]).start(priority=k)
            return c

        lax.fori_loop(0, tm, issue, 0, unroll=ROW_UNROLL)

    @pl.when(i == 0)
    def _():
        gather(dc_ref, 0)

    for s in range(2):
        @pl.when(i % 2 == s)
        def _():
            @pl.when(more)
            def _():
                gather(dn_ref, 1 - s)

            for k in range(2):
                pltpu.make_async_copy(y_ref.at[pl.ds(0, tm)], buf.at[2 * s + k], sem.at[s, k]).wait()
            ffn = buf[2 * s] * gt_ref[:, 0:1] + buf[2 * s + 1] * gt_ref[:, 1:2]
            o_ref[...] = _layer_norm(alpha * x_ref[...] + ffn, lg_ref[...], lb_ref[...])


def _combine(dest3, x1, gates_t, lg, lb, yb, alpha, tm):
    n = x1.shape[0]
    nt = n // tm
    const = lambda i: (0, 0)
    row = lambda i: (i, 0)
    return pl.pallas_call(
        functools.partial(_combine_kernel, tm=tm, alpha=alpha),
        grid=(nt,),
        in_specs=[
            pl.BlockSpec((1, 1, 2 * tm), lambda i: (i, 0, 0), memory_space=pltpu.SMEM),
            pl.BlockSpec((1, 1, 2 * tm), lambda i: (jnp.minimum(i + 1, nt - 1), 0, 0), memory_space=pltpu.SMEM),
            pl.BlockSpec((tm, D_MODEL), row),
            pl.BlockSpec((tm, 2), row),
            pl.BlockSpec(lg.shape, const),
            pl.BlockSpec(lb.shape, const),
            pl.BlockSpec(memory_space=pl.ANY),
        ],
        out_specs=pl.BlockSpec((tm, D_MODEL), row),
        out_shape=jax.ShapeDtypeStruct((n, D_MODEL), F32),
        scratch_shapes=[pltpu.VMEM((4, tm, D_MODEL), F32), pltpu.SemaphoreType.DMA((2, 2))],
        compiler_params=_cparams(("arbitrary",)),
        name="moe_combine",
    )(dest3, dest3, x1, gates_t, lg, lb, yb)


def _moe_ln(x1, e, r, gates, cnt, wg, wu, wd, layer, lg, lb, alpha, tm, tb):
    n = x1.shape[0]
    nwork = (2 * n) // tb + N_EXPERTS - 1
    nwp = -(-nwork // LANE) * LANE
    dest, blk, ex, lo, hi = _slots(cnt.astype(jnp.int32).reshape(N_EXPERTS), e, r, tb, nwp)
    dest3 = dest.reshape(2, n // tm, tm).transpose(1, 0, 2).reshape(n // tm, 1, 2 * tm)
    xs = _dispatch(dest3, x1, 2 * n, tm)
    yb = _experts(blk.reshape(nwp), ex.reshape(nwp), lo.reshape(nwp), hi.reshape(nwp), xs, wg, wu, wd,
                  layer, tb, nwork)
    return _combine(dest3, x1, gates.T, lg, lb, yb, alpha, tm)


def _sample_proj_kernel(x_ref, w_ref, gq_ref, wq_ref, gkv_ref, c_ref, s_ref,
                        q_ref, lat_ref, kpe_ref, qb_ref, kb_ref, vb_ref, acc_ref):
    k = pl.program_id(0)

    @pl.when(k == 0)
    def _():
        acc_ref[...] = jnp.zeros_like(acc_ref)

    acc_ref[...] += _dot(x_ref[...].astype(BF16), w_ref[...].astype(BF16))

    @pl.when(k == pl.num_programs(0) - 1)
    def _():
        cos = c_ref[...]
        sin = s_ref[...]
        cqn = _rms(acc_ref[:, :Q_LORA], gq_ref[...])
        q = _dot(cqn.astype(BF16), wq_ref[...])
        q_ref[:, :H_MLA * NOPE_DIM] = q[:, :H_MLA * NOPE_DIM]
        qr = q[:, H_MLA * NOPE_DIM:]
        q_ref[:, H_MLA * NOPE_DIM:] = qr * cos + _swap32(qr) * sin
        lat_ref[...] = _rms(acc_ref[:, Q_LORA:Q_LORA + KV_LORA], gkv_ref[...])
        a = Q_LORA + KV_LORA
        kv = acc_ref[:, a:a + LANE]
        kpe_ref[...] = (kv * cos[:, :LANE] + _swap32(kv) * sin[:, :LANE])[:, :ROPE_DIM]
        a += ROPE_DIM
        qb_ref[...] = acc_ref[:, a:a + BAND_WIDTH]
        kb_ref[...] = acc_ref[:, a + BAND_WIDTH:a + 2 * BAND_WIDTH]
        vb_ref[...] = acc_ref[:, a + 2 * BAND_WIDTH:a + 3 * BAND_WIDTH]


def _sample_proj(x, w_in, layer, gq, wq, gkv, cos, sin, tk):
    m = x.shape[0]
    width = w_in.shape[2]
    const = lambda k: (0, 0)
    shapes = [(m, wq.shape[1]), (m, KV_LORA), (m, ROPE_DIM), (m, BAND_WIDTH), (m, BAND_WIDTH), (m, BAND_WIDTH)]
    return pl.pallas_call(
        _sample_proj_kernel,
        grid=(D_MODEL // tk,),
        in_specs=[
            pl.BlockSpec((m, tk), lambda k: (0, k)),
            pl.BlockSpec((None, tk, width), lambda k: (layer, k, 0)),
            pl.BlockSpec(gq.shape, const),
            pl.BlockSpec(wq.shape, const),
            pl.BlockSpec(gkv.shape, const),
            pl.BlockSpec(cos.shape, const),
            pl.BlockSpec(sin.shape, const),
        ],
        out_specs=[pl.BlockSpec(s, const) for s in shapes],
        out_shape=[jax.ShapeDtypeStruct(s, F32) for s in shapes],
        scratch_shapes=[pltpu.VMEM((m, width), F32)],
        compiler_params=_cparams(("arbitrary",)),
        name="sample_proj",
    )(x, w_in, gq, wq, gkv, cos, sin)


def _softmax2(sc, sn):
    m = jnp.maximum(jnp.max(sc, axis=-1, keepdims=True), jnp.max(sn, axis=-1, keepdims=True))
    ec = jnp.exp(sc - m)
    en = jnp.exp(sn - m)
    l = jnp.sum(ec, axis=-1, keepdims=True) + jnp.sum(en, axis=-1, keepdims=True)
    return (ec / l).astype(BF16), (en / l).astype(BF16)


def _sample_mla_kernel(q_ref, lat_ref, kpe_ref, cl_ref, cr_ref, wk_ref, wv_ref, o_ref, *, scale):
    lat_c = cl_ref[...].astype(BF16)
    lat_n = lat_ref[...].astype(BF16)
    kpe_c = cr_ref[...].astype(BF16)
    kpe_n = kpe_ref[...].astype(BF16)
    kc = _dot(lat_c, wk_ref[...]).astype(BF16)
    vc = _dot(lat_c, wv_ref[...]).astype(BF16)
    kn = _dot(lat_n, wk_ref[...]).astype(BF16)
    vn = _dot(lat_n, wv_ref[...]).astype(BF16)
    for hd in range(H_MLA):
        a = hd * NOPE_DIM
        b = H_MLA * NOPE_DIM + hd * ROPE_DIM
        qn = q_ref[:, a:a + NOPE_DIM].astype(BF16)
        qr = q_ref[:, b:b + ROPE_DIM].astype(BF16)
        sc = (_dot_nt(qn, kc[:, a:a + NOPE_DIM]) + _dot_nt(qr, kpe_c)) * scale
        sn = (_dot_nt(qn, kn[:, a:a + NOPE_DIM]) + _dot_nt(qr, kpe_n)) * scale
        pc, pn = _softmax2(sc, sn)
        o_ref[:, a:a + V_DIM] = _dot(pc, vc[:, a:a + V_DIM]) + _dot(pn, vn[:, a:a + V_DIM])


def _sample_mla(q, lat, kpe, cache_lat, cache_rope, wk, wv, layer, t):
    m = q.shape[0]
    batch = m // t
    past = cache_lat.shape[2]
    row = lambda b: (b, 0)
    const = lambda b: (0, 0)
    return pl.pallas_call(
        functools.partial(_sample_mla_kernel, scale=(NOPE_DIM + ROPE_DIM) ** -0.5),
        grid=(batch,),
        in_specs=[
            pl.BlockSpec((t, q.shape[1]), row),
            pl.BlockSpec((t, KV_LORA), row),
            pl.BlockSpec((t, ROPE_DIM), row),
            pl.BlockSpec((None, None, past, KV_LORA), lambda b: (layer, b, 0, 0)),
            pl.BlockSpec((None, None, past, ROPE_DIM), lambda b: (layer, b, 0, 0)),
            pl.BlockSpec((None, KV_LORA, MLA_WIDTH), lambda b: (layer, 0, 0)),
            pl.BlockSpec((None, KV_LORA, MLA_WIDTH), lambda b: (layer, 0, 0)),
        ],
        out_specs=pl.BlockSpec((t, MLA_WIDTH), row),
        out_shape=jax.ShapeDtypeStruct((m, MLA_WIDTH), F32),
        compiler_params=_cparams(("parallel",)),
        name="sample_mla",
    )(q, lat, kpe, cache_lat, cache_rope, wk, wv)


def _sample_band_kernel(q_ref, kn_ref, vn_ref, ck_ref, cv_ref, bc_ref, bn_ref, o_ref, *, scale):
    for hd in range(H_BAND):
        a = hd * D_BAND
        q = q_ref[:, a:a + D_BAND].astype(BF16)
        sc = _dot_nt(q, ck_ref[:, a:a + D_BAND].astype(BF16)) * scale + bc_ref[hd]
        sn = _dot_nt(q, kn_ref[:, a:a + D_BAND].astype(BF16)) * scale + bn_ref[hd]
        pc, pn = _softmax2(sc, sn)
        o_ref[:, a:a + D_BAND] = (_dot(pc, cv_ref[:, a:a + D_BAND].astype(BF16))
                                  + _dot(pn, vn_ref[:, a:a + D_BAND].astype(BF16)))


def _sample_band(q, kn, vn, cache_k, cache_v, bias_c, bias_n, layer, t):
    m = q.shape[0]
    batch = m // t
    r = cache_k.shape[2]
    row = lambda b: (b, 0)
    cache = lambda b: (layer, b, 0, 0)
    return pl.pallas_call(
        functools.partial(_sample_band_kernel, scale=D_BAND ** -0.5),
        grid=(batch,),
        in_specs=[
            pl.BlockSpec((t, BAND_WIDTH), row),
            pl.BlockSpec((t, BAND_WIDTH), row),
            pl.BlockSpec((t, BAND_WIDTH), row),
            pl.BlockSpec((None, None, r, BAND_WIDTH), cache),
            pl.BlockSpec((None, None, r, BAND_WIDTH), cache),
            pl.BlockSpec(bias_c.shape, lambda b: (0, 0, 0)),
            pl.BlockSpec(bias_n.shape, lambda b: (0, 0, 0)),
        ],
        out_specs=pl.BlockSpec((t, BAND_WIDTH), row),
        out_shape=jax.ShapeDtypeStruct((m, BAND_WIDTH), F32),
        compiler_params=_cparams(("parallel",)),
        name="sample_band",
    )(q, kn, vn, cache_k, cache_v, bias_c, bias_n)


def _spread(w):
    z = jnp.zeros(w.shape[:-1] + (HALF_ROPE,), w.dtype)
    return jnp.concatenate([w[..., :HALF_ROPE], z, w[..., HALF_ROPE:], z], axis=-1)


def _rope_angles(pos):
    inv = 1.0 / (ROPE_THETA ** (jnp.arange(0, ROPE_DIM, 2, dtype=F32) / ROPE_DIM))
    ang = pos.astype(F32)[:, None] * inv[None, :]
    return jnp.cos(ang), jnp.sin(ang)


def _band_bias_kernel(line_ref, o_ref, *, q0, k0, keys_first):
    nr, nc = o_ref.shape
    table = pltpu.roll(jnp.broadcast_to(line_ref[...], (nr, line_ref.shape[1])), 0, 1, stride=1, stride_axis=0)
    row = lax.broadcasted_iota(jnp.int32, (nr, nc), 0)
    col = lax.broadcasted_iota(jnp.int32, (nr, nc), 1)
    shift = CHUNK * (1 << 20)
    kc = ((row if keys_first else col) + (k0 + shift)) // CHUNK
    qc = ((col if keys_first else row) + (q0 + shift)) // CHUNK
    o_ref[...] = jnp.where((kc <= qc) & (kc >= qc - LEFT_CHUNKS), table[:, :nc], NEG)


def _band_bias_table(rel_bias, q0, nq, k0, nk, keys_first):
    heads = rel_bias.shape[0]
    nr, nc = (nk, nq) if keys_first else (nq, nk)
    period = -(-(nr + nc) // LANE) * LANE
    delta = np.arange(period)
    delta = np.where(delta < nc, delta, delta - period)
    qk = (q0 - k0) + (delta if keys_first else -delta)
    line = rel_bias[:, np.clip(qk, -REL_CLIP, REL_CLIP) + REL_CLIP].reshape(heads, 1, period)
    return pl.pallas_call(
        functools.partial(_band_bias_kernel, q0=q0, k0=k0, keys_first=keys_first),
        grid=(heads,),
        in_specs=[pl.BlockSpec((None, 1, period), lambda h: (h, 0, 0))],
        out_specs=pl.BlockSpec((None, nr, nc), lambda h: (h, 0, 0)),
        out_shape=jax.ShapeDtypeStruct((heads, nr, nc), F32),
        compiler_params=_cparams(("parallel",)),
        name="band_bias",
    )(line)


def kernel(x_prompt, x_sample, cache_mla_latent, cache_mla_rope, cache_band_k, cache_band_v, w_in, g_q_lat, w_q_up, g_kv_lat, w_k_up, w_v_up, rel_bias, g_out_mla, g_out_band, w_out, ln1_g, ln1_b, w_router, b_router, w_gate, w_up, w_down, ln2_g, ln2_b):
    depth = w_in.shape[0]
    batch, seq, _ = x_prompt.shape
    dec_batch, dec_seq, _ = x_sample.shape
    past = cache_mla_latent.shape[2]
    band_r = cache_band_k.shape[2]
    alpha = float((2 * depth) ** 0.25)
    n = batch * seq
    m = dec_batch * dec_seq
    mla_scale = (NOPE_DIM + ROPE_DIM) ** -0.5 * LOG2E
    band_scale = D_BAND ** -0.5 * LOG2E
    a0 = Q_LORA + KV_LORA
    a1 = a0 + ROPE_DIM

    w_a = jnp.concatenate([w_in[:, :, :a0], _spread(w_in[:, :, a0:a1])], axis=-1).astype(BF16)
    w_band = jnp.concatenate([w_in[:, :, a1:a1 + BAND_WIDTH] * band_scale, w_in[:, :, a1 + BAND_WIDTH:]],
                             axis=-1).astype(BF16)
    w_band_vt = w_in[:, :, a1 + 2 * BAND_WIDTH:].transpose(0, 2, 1).astype(BF16)
    wq4 = w_q_up.reshape(depth, Q_LORA, H_MLA, NOPE_DIM + ROPE_DIM)
    wq_pad = jnp.concatenate([wq4[..., :NOPE_DIM], _spread(wq4[..., NOPE_DIM:])], axis=-1) * mla_scale
    wq_pad = wq_pad.reshape(depth, Q_LORA, H_MLA * QK_PAD).astype(BF16)
    wq_s = jnp.concatenate([wq4[..., :NOPE_DIM].reshape(depth, Q_LORA, -1),
                            wq4[..., NOPE_DIM:].reshape(depth, Q_LORA, -1)], axis=-1).astype(BF16)
    wk4 = w_k_up.reshape(depth, KV_LORA, H_MLA, NOPE_DIM)
    wk_pad = jnp.concatenate([wk4, jnp.zeros_like(wk4)], axis=-1).reshape(depth, KV_LORA, H_MLA * QK_PAD)
    wk_pad = wk_pad.astype(BF16)
    wk_b = w_k_up.astype(BF16)
    wv_b = w_v_up.astype(BF16)
    wv_t = w_v_up.transpose(0, 2, 1).astype(BF16)
    w_out_b = w_out.astype(BF16)
    wg_b = w_gate.astype(BF16)
    wu_b = w_up.astype(BF16)
    wd_b = w_down.astype(BF16)
    wr_tb = w_router.T.astype(BF16)
    br = b_router.reshape(N_EXPERTS, 1)
    row2 = lambda a: a.reshape(depth, 1, -1)
    gq, gkv, ga, gb = row2(g_q_lat), row2(g_kv_lat), row2(g_out_mla), row2(g_out_band)
    l1g, l1b, l2g, l2b = row2(ln1_g), row2(ln1_b), row2(ln2_g), row2(ln2_b)

    pos_p = jnp.arange(seq, dtype=jnp.int32)
    cos_p, sin_p = _rope_angles(pos_p)
    z = jnp.zeros_like(cos_p)
    cos_sp = jnp.concatenate([cos_p, z, cos_p, z], axis=-1)
    sin_sp = jnp.concatenate([-sin_p, z, sin_p, z], axis=-1)
    pos_s = past + jnp.arange(dec_seq, dtype=jnp.int32)
    cos_s, sin_s = _rope_angles(pos_s)
    cos_s = jnp.tile(jnp.concatenate([cos_s, cos_s], axis=-1), (dec_batch, H_MLA))
    sin_s = jnp.tile(jnp.concatenate([-sin_s, sin_s], axis=-1), (dec_batch, H_MLA))

    tm_p = 256
    states = [[], [], [], [], [], [], [], []]
    xp = x_prompt.reshape(n, D_MODEL)
    xs = x_sample.reshape(m, D_MODEL)
    ck = cache_band_k.reshape(depth, dec_batch, band_r, BAND_WIDTH)
    cv = cache_band_v.reshape(depth, dec_batch, band_r, BAND_WIDTH)
    for l in range(depth):
        q, k, vt, lat, kpe = _mla_proj(xp, w_a[l], gq[l], wq_pad[l], gkv[l], wk_pad[l], wv_t[l],
                                       cos_sp, sin_sp, seq, tm_p, BAND_PAST)
        qb, kb, vbt, ks, vs = _band_proj(xp, w_band[l], w_band_vt[l], seq, tm_p)
        oa = _mla_attn(q, k, vt, seq, BAND_PAST)
        bias_p = _band_bias_table(rel_bias[l] * LOG2E, 0, BAND_PAST, -BAND_PAST, 2 * BAND_PAST, True)
        ob = _band_attn(qb, kb, vbt, bias_p, seq)
        x1, e, r, gates, cnt = _out_route(oa, ob, ga[l], gb[l], w_out_b[l], xp, l1g[l], l1b[l], wr_tb, br,
                                          alpha, 2 * tm_p, tm_p)
        xp = _moe_ln(x1, e, r, gates, cnt, wg_b, wu_b, wd_b, l, l2g[l], l2b[l], alpha, tm_p, 256)
        for acc, s in zip(states[:4], (lat, kpe, ks, vs)):
            acc.append(s)
        sq, slat, skpe, sqb, skb, svb = _sample_proj(xs, w_in, l, gq[l], wq_s[l], gkv[l], cos_s, sin_s, 256)
        soa = _sample_mla(sq, slat, skpe, cache_mla_latent, cache_mla_rope, wk_b, wv_b, l, dec_seq)
        bias_c = _band_bias_table(rel_bias[l], past, dec_seq, past - band_r, band_r, False)
        bias_n = _band_bias_table(rel_bias[l], past, dec_seq, past, dec_seq, False)
        sob = _sample_band(sqb, skb, svb, ck, cv, bias_c, bias_n, l, dec_seq)
        sx1, se, sr, sgates, scnt = _out_route(soa, sob, ga[l], gb[l], w_out_b[l], xs, l1g[l], l1b[l], wr_tb, br,
                                               alpha, m, m)
        xs = _moe_ln(sx1, se, sr, sgates, scnt, wg_b, wu_b, wd_b, l, l2g[l], l2b[l], alpha, m, 128)
        for acc, s in zip(states[4:], (slat, skpe, skb, svb)):
            acc.append(s)

    st = [jnp.stack(a) for a in states]
    return (xp.reshape(batch, seq, D_MODEL), xs.reshape(dec_batch, dec_seq, D_MODEL),
            st[0].reshape(depth, batch, seq, KV_LORA), st[1].reshape(depth, batch, seq, ROPE_DIM),
            st[2].reshape(depth, batch, BAND_PAST, H_BAND, D_BAND),
            st[3].reshape(depth, batch, BAND_PAST, H_BAND, D_BAND),
            st[4].reshape(depth, dec_batch, dec_seq, KV_LORA), st[5].reshape(depth, dec_batch, dec_seq, ROPE_DIM),
            st[6].reshape(depth, dec_batch, dec_seq, H_BAND, D_BAND),
            st[7].reshape(depth, dec_batch, dec_seq, H_BAND, D_BAND))
```

```python
import functools

import jax
import jax.numpy as jnp
import numpy as np
from jax import lax
from jax.experimental import pallas as pl
from jax.experimental.pallas import tpu as pltpu

D_MODEL = 2048
CHUNK = 64
H_MLA = 8
NOPE_DIM = 128
ROPE_DIM = 64
HALF_ROPE = ROPE_DIM // 2
V_DIM = 128
Q_LORA = 512
KV_LORA = 256
ROPE_THETA = 10000.0
H_BAND = 8
D_BAND = 128
LEFT_CHUNKS = 8
BAND_PAST = LEFT_CHUNKS * CHUNK
REL_CLIP = 128
MLA_WIDTH = H_MLA * V_DIM
BAND_WIDTH = H_BAND * D_BAND
QK_PAD = 256
N_EXPERTS = 16
N_GROUPS = 4
GROUP_SIZE = 4
EXPERT_FF = 512
RMS_EPS = 1e-6
LN_EPS = 1e-5
NEG = -1e30
LOG2E = 1.4426950408889634
LANE = 128
VMEM_LIMIT = 56 * 1024 * 1024

BF16_ROWS = 16
VT_ROWS = V_DIM + BF16_ROWS
ROW_UNROLL = 8

F32 = jnp.float32
BF16 = jnp.bfloat16


def _cparams(sem):
    return pltpu.CompilerParams(dimension_semantics=sem, vmem_limit_bytes=VMEM_LIMIT)


def _dot(a, b):
    return jnp.dot(a, b, preferred_element_type=F32)


def _dot_nt(a, b):
    return lax.dot_general(a, b, (((1,), (1,)), ((), ())), preferred_element_type=F32)


def _rms(x, g):
    return x * lax.rsqrt(jnp.mean(x * x, axis=-1, keepdims=True) + RMS_EPS) * g


def _layer_norm(x, g, b):
    mu = jnp.mean(x, axis=-1, keepdims=True)
    xc = x - mu
    var = jnp.mean(xc * xc, axis=-1, keepdims=True)
    return xc * lax.rsqrt(var + LN_EPS) * g + b


def _swap32(x):
    w = x.shape[-1]
    lane = lax.broadcasted_iota(jnp.int32, x.shape, x.ndim - 1)
    return jnp.where((lane % ROPE_DIM) < HALF_ROPE, pltpu.roll(x, w - HALF_ROPE, x.ndim - 1),
                     pltpu.roll(x, HALF_ROPE, x.ndim - 1))


def _mla_proj_kernel(x_ref, wa_ref, gq_ref, wq_ref, gkv_ref, wk_ref, wv_ref, c_ref, s_ref,
                     q_ref, k_ref, v_ref, lat_ref, kpe_ref):
    xb = x_ref[...].astype(BF16)
    h = _dot(xb, wa_ref[...])
    cqn = _rms(h[:, :Q_LORA], gq_ref[...])
    lat = _rms(h[:, Q_LORA:Q_LORA + KV_LORA], gkv_ref[...])
    kp = h[:, Q_LORA + KV_LORA:]
    cos = c_ref[...]
    sin = s_ref[...]
    kpr = kp * cos + pltpu.roll(kp, 2 * HALF_ROPE, 1) * sin
    lat_ref[...] = lat
    kpe_ref[...] = (kpr + pltpu.roll(kpr, LANE - HALF_ROPE, 1))[:, :ROPE_DIM]
    kprb = kpr.astype(BF16)
    latb = lat.astype(BF16)
    q = _dot(cqn.astype(BF16), wq_ref[...])
    kn = _dot(latb, wk_ref[...])
    vt = _dot_nt(wv_ref[...], latb).astype(BF16)
    ones = jnp.ones((BF16_ROWS, vt.shape[1]), BF16)
    for hd in range(H_MLA):
        v_ref[hd * VT_ROWS:hd * VT_ROWS + V_DIM, :] = vt[hd * V_DIM:(hd + 1) * V_DIM, :]
        v_ref[hd * VT_ROWS + V_DIM:(hd + 1) * VT_ROWS, :] = ones
    for hd in range(H_MLA):
        a = hd * QK_PAD
        qv = q[:, a + NOPE_DIM:a + QK_PAD]
        q_ref[:, a:a + NOPE_DIM] = q[:, a:a + NOPE_DIM].astype(BF16)
        q_ref[:, a + NOPE_DIM:a + QK_PAD] = (qv * cos + pltpu.roll(qv, 2 * HALF_ROPE, 1) * sin).astype(BF16)
        k_ref[:, a:a + NOPE_DIM] = kn[:, a:a + NOPE_DIM].astype(BF16)
        k_ref[:, a + NOPE_DIM:a + QK_PAD] = kprb


def _mla_proj(x, wa, gq, wq, gkv, wk, wv, cos, sin, seq, tm, tk):
    n = x.shape[0]
    tpb = seq // tm
    per_kv = tk // tm
    const = lambda i: (0, 0)
    row = lambda i: (i, 0)
    return pl.pallas_call(
        _mla_proj_kernel,
        grid=(n // tm,),
        in_specs=[
            pl.BlockSpec((tm, D_MODEL), row),
            pl.BlockSpec(wa.shape, const),
            pl.BlockSpec(gq.shape, const),
            pl.BlockSpec(wq.shape, const),
            pl.BlockSpec(gkv.shape, const),
            pl.BlockSpec(wk.shape, const),
            pl.BlockSpec(wv.shape, const),
            pl.BlockSpec((tm, LANE), lambda i: (i % tpb, 0)),
            pl.BlockSpec((tm, LANE), lambda i: (i % tpb, 0)),
        ],
        out_specs=[
            pl.BlockSpec((tm, H_MLA * QK_PAD), row),
            pl.BlockSpec((tm, H_MLA * QK_PAD), row),
            pl.BlockSpec((None, H_MLA * VT_ROWS, tm), lambda i: (i // per_kv, 0, i % per_kv)),
            pl.BlockSpec((tm, KV_LORA), row),
            pl.BlockSpec((tm, ROPE_DIM), row),
        ],
        out_shape=[
            jax.ShapeDtypeStruct((n, H_MLA * QK_PAD), BF16),
            jax.ShapeDtypeStruct((n, H_MLA * QK_PAD), BF16),
            jax.ShapeDtypeStruct((n // tk, H_MLA * VT_ROWS, tk), BF16),
            jax.ShapeDtypeStruct((n, KV_LORA), F32),
            jax.ShapeDtypeStruct((n, ROPE_DIM), F32),
        ],
        compiler_params=_cparams(("parallel",)),
        name="mla_proj",
    )(x, wa, gq, wq, gkv, wk, wv, cos, sin)


def _band_proj_kernel(x_ref, w_ref, wvt_ref, q_ref, k_ref, v_ref, ks_ref, vs_ref, *, tpb, ntail):
    xb = x_ref[...].astype(BF16)
    q_ref[...] = _dot(xb, w_ref[:, :BAND_WIDTH]).astype(BF16)
    k = _dot(xb, w_ref[:, BAND_WIDTH:2 * BAND_WIDTH])
    k_ref[...] = k.astype(BF16)
    vt = _dot_nt(wvt_ref[...], xb).astype(BF16)
    ones = jnp.ones((BF16_ROWS, vt.shape[1]), BF16)
    for hd in range(H_BAND):
        v_ref[hd * VT_ROWS:hd * VT_ROWS + D_BAND, :] = vt[hd * D_BAND:(hd + 1) * D_BAND, :]
        v_ref[hd * VT_ROWS + D_BAND:(hd + 1) * VT_ROWS, :] = ones

    @pl.when(pl.program_id(0) % tpb >= tpb - ntail)
    def _():
        ks_ref[...] = k
        vs_ref[...] = _dot(xb, w_ref[:, 2 * BAND_WIDTH:])


def _band_proj(x, w, wvt, seq, tm):
    n = x.shape[0]
    tpb = seq // tm
    ntail = BAND_PAST // tm
    batch = n // seq
    const = lambda i: (0, 0)
    row = lambda i: (i, 0)
    tail = lambda i: (i // tpb, jnp.maximum(i % tpb - (tpb - ntail), 0), 0)
    return pl.pallas_call(
        functools.partial(_band_proj_kernel, tpb=tpb, ntail=ntail),
        grid=(n // tm,),
        in_specs=[pl.BlockSpec((tm, D_MODEL), row), pl.BlockSpec(w.shape, const), pl.BlockSpec(wvt.shape, const)],
        out_specs=[
            pl.BlockSpec((tm, BAND_WIDTH), row),
            pl.BlockSpec((tm, BAND_WIDTH), row),
            pl.BlockSpec((None, H_BAND * VT_ROWS, tm), lambda i: (i // ntail, 0, i % ntail)),
            pl.BlockSpec((None, tm, BAND_WIDTH), tail),
            pl.BlockSpec((None, tm, BAND_WIDTH), tail),
        ],
        out_shape=[
            jax.ShapeDtypeStruct((n, BAND_WIDTH), BF16),
            jax.ShapeDtypeStruct((n, BAND_WIDTH), BF16),
            jax.ShapeDtypeStruct((n // BAND_PAST, H_BAND * VT_ROWS, BAND_PAST), BF16),
            jax.ShapeDtypeStruct((batch, BAND_PAST, BAND_WIDTH), F32),
            jax.ShapeDtypeStruct((batch, BAND_PAST, BAND_WIDTH), F32),
        ],
        compiler_params=_cparams(("arbitrary",)),
        name="band_proj",
    )(x, w, wvt)


def _attn_out(acc):
    return (acc[:V_DIM, :] / acc[V_DIM:V_DIM + 1, :]).T


def _mla_attn_kernel(qt_ref, jt_ref, q_ref, k_ref, vt_ref, o_ref, m_ref, acc_ref, sa_ref, sb_ref, ma_ref, mb_ref,
                     *, tk, nblk):
    tq = 2 * tk
    kc = lax.broadcasted_iota(jnp.int32, (tk, tk), 0) // CHUNK
    qc = lax.broadcasted_iota(jnp.int32, (tk, tk), 1) // CHUNK
    causal = kc <= qc

    def produce(t, buf, lanes):
        s_ref, bm_ref = buf
        t = jnp.minimum(t, nblk - 1)
        qo = pl.multiple_of(qt_ref[t] * tq + lanes.start, tk)
        ko = pl.multiple_of(jt_ref[t] * tk, tk)
        st = _dot_nt(k_ref[pl.ds(ko, tk), :], q_ref[pl.ds(qo, lanes.stop - lanes.start), :])
        s_ref[:, lanes] = st
        bm_ref[:, lanes] = jnp.max(st, axis=0, keepdims=True)

    def fold(j, st, bmax, lanes):
        m = m_ref[:, lanes]
        m_new = jnp.maximum(m, bmax)
        p = jnp.exp2(st - m_new)
        acc_ref[:, lanes] = jnp.exp2(m - m_new) * acc_ref[:, lanes] + _dot(vt_ref[j], p.astype(BF16))
        m_ref[:, lanes] = m_new

    everything = slice(0, tq)
    lower = slice(0, tk)
    upper = slice(tk, tq)

    def step(t, cur, nxt):
        qi = qt_ref[t]
        j = jt_ref[t]

        @pl.when(j < 2 * qi)
        def _():
            produce(t + 1, nxt, everything)
            fold(j, cur[0][...], cur[1][...], everything)

        @pl.when(j == 2 * qi)
        def _():
            produce(t + 1, nxt, upper)
            st = jnp.where(causal, cur[0][:, lower], NEG)
            fold(j, st, jnp.max(st, axis=0, keepdims=True), lower)
            fold(j, cur[0][:, upper], cur[1][:, upper], upper)

        @pl.when(j == 2 * qi + 1)
        def _():
            produce(t + 1, nxt, everything)
            st = jnp.where(causal, cur[0][:, upper], NEG)
            fold(j, st, jnp.max(st, axis=0, keepdims=True), upper)
            qo = pl.multiple_of(qi * tq, tq)
            o_ref[pl.ds(qo, tq), :] = _attn_out(acc_ref[...]).astype(o_ref.dtype)
            m_ref[...] = jnp.full(m_ref.shape, NEG, F32)
            acc_ref[...] = jnp.zeros(acc_ref.shape, F32)

    m_ref[...] = jnp.full(m_ref.shape, NEG, F32)
    acc_ref[...] = jnp.zeros(acc_ref.shape, F32)
    buf_a = (sa_ref, ma_ref)
    buf_b = (sb_ref, mb_ref)
    produce(0, buf_a, everything)

    def pair(i, c):
        step(2 * i, buf_a, buf_b)
        step(2 * i + 1, buf_b, buf_a)
        return c

    lax.fori_loop(0, nblk // 2, pair, 0)


def _mla_attn(q, k, vt, seq, tk):
    n = q.shape[0]
    batch = n // seq
    tq = 2 * tk
    assert seq % tq == 0
    pairs = [(i, j) for i in range(seq // tq) for j in range(2 * i + 2)]
    qt = jnp.asarray([p[0] for p in pairs], jnp.int32)
    jt = jnp.asarray([p[1] for p in pairs], jnp.int32)
    blk = lambda b, h, qt, jt: (b, h)
    return pl.pallas_call(
        functools.partial(_mla_attn_kernel, tk=tk, nblk=len(pairs)),
        grid_spec=pltpu.PrefetchScalarGridSpec(
            num_scalar_prefetch=2,
            grid=(batch, H_MLA),
            in_specs=[
                pl.BlockSpec((seq, QK_PAD), blk),
                pl.BlockSpec((seq, QK_PAD), blk),
                pl.BlockSpec((seq // tk, VT_ROWS, tk), lambda b, h, qt, jt: (b, h, 0)),
            ],
            out_specs=pl.BlockSpec((seq, V_DIM), blk),
            scratch_shapes=[pltpu.VMEM((1, tq), F32), pltpu.VMEM((VT_ROWS, tq), F32),
                            pltpu.VMEM((tk, tq), F32), pltpu.VMEM((tk, tq), F32),
                            pltpu.VMEM((1, tq), F32), pltpu.VMEM((1, tq), F32)],
        ),
        out_shape=jax.ShapeDtypeStruct((n, MLA_WIDTH), BF16),
        compiler_params=_cparams(("parallel", "parallel")),
        name="mla_attn",
    )(qt, jt, q, k, vt)


def _band_attn_kernel(q_ref, k_ref, vt_ref, b_ref, o_ref, *, nq):
    tq = BAND_PAST
    for i in range(nq):
        q = q_ref[i * tq:(i + 1) * tq, :]
        sc = _dot_nt(k_ref[i * tq:(i + 1) * tq, :], q) + b_ref[tq:, :]
        m = jnp.max(sc, axis=0, keepdims=True)
        if i > 0:
            sp = _dot_nt(k_ref[(i - 1) * tq:i * tq, :], q) + b_ref[:tq, :]
            m = jnp.maximum(m, jnp.max(sp, axis=0, keepdims=True))
            acc = _dot(vt_ref[i - 1], jnp.exp2(sp - m).astype(BF16))
            acc = acc + _dot(vt_ref[i], jnp.exp2(sc - m).astype(BF16))
        else:
            acc = _dot(vt_ref[i], jnp.exp2(sc - m).astype(BF16))
        o_ref[i * tq:(i + 1) * tq, :] = _attn_out(acc).astype(o_ref.dtype)


def _band_attn(q, k, vt, bias_t, seq):
    n = q.shape[0]
    batch = n // seq
    nq = seq // BAND_PAST
    blk = lambda b, h: (b, h)
    return pl.pallas_call(
        functools.partial(_band_attn_kernel, nq=nq),
        grid=(batch, H_BAND),
        in_specs=[
            pl.BlockSpec((seq, D_BAND), blk),
            pl.BlockSpec((seq, D_BAND), blk),
            pl.BlockSpec((nq, VT_ROWS, BAND_PAST), lambda b, h: (b, h, 0)),
            pl.BlockSpec((None, 2 * BAND_PAST, BAND_PAST), lambda b, h: (h, 0, 0)),
        ],
        out_specs=pl.BlockSpec((seq, D_BAND), blk),
        out_shape=jax.ShapeDtypeStruct((n, BAND_WIDTH), BF16),
        compiler_params=_cparams(("parallel", "parallel")),
        name="band_attn",
    )(q, k, vt, bias_t)


def _top2_of4(v, sv):
    best, bi, bs = v[0], jnp.zeros(v[0].shape, jnp.int32), sv[0]
    for i in range(1, GROUP_SIZE):
        c = v[i] > best
        best = jnp.where(c, v[i], best)
        bi = jnp.where(c, i, bi)
        bs = jnp.where(c, sv[i], bs)
    sec = jnp.full(v[0].shape, -jnp.inf, F32)
    si = jnp.zeros(v[0].shape, jnp.int32)
    ss = jnp.zeros(v[0].shape, F32)
    for i in range(GROUP_SIZE):
        c = (bi != i) & (v[i] > sec)
        sec = jnp.where(c, v[i], sec)
        si = jnp.where(c, i, si)
        ss = jnp.where(c, sv[i], ss)
    return best + sec, bi, si, bs, ss


def _route(sel, s):
    groups = []
    for g in range(N_GROUPS):
        rows = range(g * GROUP_SIZE, (g + 1) * GROUP_SIZE)
        groups.append(_top2_of4([sel[i:i + 1, :] for i in rows], [s[i:i + 1, :] for i in rows]))
    score, bi, si, bs, ss = groups[0]
    gi = jnp.zeros(score.shape, jnp.int32)
    for g in range(1, N_GROUPS):
        c = groups[g][0] > score
        score = jnp.where(c, groups[g][0], score)
        gi = jnp.where(c, g, gi)
        bi = jnp.where(c, groups[g][1], bi)
        si = jnp.where(c, groups[g][2], si)
        bs = jnp.where(c, groups[g][3], bs)
        ss = jnp.where(c, groups[g][4], ss)
    den = bs + ss
    return gi * GROUP_SIZE + bi, gi * GROUP_SIZE + si, bs / den, ss / den


def _out_route_kernel(oa_ref, ob_ref, ga_ref, gb_ref, wo_ref, x_ref, lg_ref, lb_ref, wr_ref, br_ref, u_ref,
                      x1_ref, e_ref, r_ref, g_ref, cnt_ref, base_ref, *, alpha):
    @pl.when(pl.program_id(0) == 0)
    def _():
        base_ref[...] = jnp.zeros_like(base_ref)

    ts = u_ref.shape[0]
    base = base_ref[...]
    for sub in range(x_ref.shape[0] // ts):
        rows = slice(sub * ts, (sub + 1) * ts)
        na = _rms(oa_ref[rows, :].astype(F32), ga_ref[...]).astype(BF16)
        nb = _rms(ob_ref[rows, :].astype(F32), gb_ref[...]).astype(BF16)
        mix = _dot(na, wo_ref[:MLA_WIDTH, :]) + _dot(nb, wo_ref[MLA_WIDTH:, :])
        x1 = _layer_norm(alpha * x_ref[rows, :] + mix, lg_ref[...], lb_ref[...])
        x1_ref[rows, :] = x1
        logits = _dot_nt(wr_ref[...], x1.astype(BF16))
        s = 1.0 / (1.0 + jnp.exp(-logits))
        e0, e1, g0, g1 = _route(s + br_ref[...], s)
        e_ref[0:1, rows] = e0
        e_ref[1:2, rows] = e1
        g_ref[0:1, rows] = g0
        g_ref[1:2, rows] = g1
        eid = lax.broadcasted_iota(jnp.int32, logits.shape, 0)
        oh0 = eid == e0
        oh1 = eid == e1
        both = jnp.where(oh0 | oh1, 1.0, 0.0)
        before = _dot(both.astype(BF16), u_ref[...]) + base
        r_ref[0:1, rows] = jnp.sum(jnp.where(oh0, before, 0.0), axis=0, keepdims=True).astype(jnp.int32)
        r_ref[1:2, rows] = jnp.sum(jnp.where(oh1, before, 0.0), axis=0, keepdims=True).astype(jnp.int32)
        base = base + jnp.sum(both, axis=1, keepdims=True)
    base_ref[...] = base
    cnt_ref[...] = base


def _out_route(oa, ob, ga, gb, wo, x, lg, lb, wr, br, alpha, tm, ts):
    n = x.shape[0]
    u = (lax.broadcasted_iota(jnp.int32, (ts, ts), 0) < lax.broadcasted_iota(jnp.int32, (ts, ts), 1)).astype(BF16)
    const = lambda i: (0, 0)
    row = lambda i: (i, 0)
    col = lambda i: (0, i)
    return pl.pallas_call(
        functools.partial(_out_route_kernel, alpha=alpha),
        grid=(n // tm,),
        in_specs=[
            pl.BlockSpec((tm, MLA_WIDTH), row),
            pl.BlockSpec((tm, BAND_WIDTH), row),
            pl.BlockSpec(ga.shape, const),
            pl.BlockSpec(gb.shape, const),
            pl.BlockSpec(wo.shape, const),
            pl.BlockSpec((tm, D_MODEL), row),
            pl.BlockSpec(lg.shape, const),
            pl.BlockSpec(lb.shape, const),
            pl.BlockSpec(wr.shape, const),
            pl.BlockSpec(br.shape, const),
            pl.BlockSpec((ts, ts), const),
        ],
        out_specs=[
            pl.BlockSpec((tm, D_MODEL), row),
            pl.BlockSpec((2, tm), col),
            pl.BlockSpec((2, tm), col),
            pl.BlockSpec((2, tm), col),
            pl.BlockSpec((N_EXPERTS, 1), const),
        ],
        out_shape=[
            jax.ShapeDtypeStruct((n, D_MODEL), F32),
            jax.ShapeDtypeStruct((2, n), jnp.int32),
            jax.ShapeDtypeStruct((2, n), jnp.int32),
            jax.ShapeDtypeStruct((2, n), F32),
            jax.ShapeDtypeStruct((N_EXPERTS, 1), F32),
        ],
        scratch_shapes=[pltpu.VMEM((N_EXPERTS, 1), F32)],
        compiler_params=_cparams(("arbitrary",)),
        name="out_route",
    )(oa, ob, ga, gb, wo, x, lg, lb, wr, br, u)


def _slots_kernel(cnt_ref, e_ref, r_ref, d_ref, blk_ref, ex_ref, lo_ref, hi_ref, *, tb, nblocks):
    shift = tb.bit_length() - 1
    e = e_ref[...]
    dest = r_ref[...]
    item = lax.broadcasted_iota(jnp.int32, blk_ref.shape, 1)
    blk = jnp.zeros(blk_ref.shape, jnp.int32)
    exv = jnp.zeros(blk_ref.shape, jnp.int32)
    lo = jnp.zeros(blk_ref.shape, jnp.int32)
    hi = jnp.zeros(blk_ref.shape, jnp.int32)
    start = jnp.int32(0)
    first = jnp.int32(0)
    last_e = jnp.int32(0)
    for ex in range(N_EXPERTS):
        cnt = cnt_ref[ex]
        end = start + cnt
        sb = lax.shift_right_logical(start, shift)
        nitems = jnp.where(cnt > 0, lax.shift_right_logical(end - 1, shift) - sb + 1, 0)
        dest = dest + jnp.where(e == ex, start, 0)
        mine = (item >= first) & (item < first + nitems)
        b = sb + item - first
        blk = jnp.where(mine, b, blk)
        exv = jnp.where(mine, ex, exv)
        lo = jnp.where(mine, jnp.maximum(start, b * tb) - b * tb, lo)
        hi = jnp.where(mine, jnp.minimum(end, (b + 1) * tb) - b * tb, hi)
        last_e = jnp.where(cnt > 0, ex, last_e)
        start = end
        first = first + nitems
    unused = item >= first
    d_ref[...] = dest
    blk_ref[...] = jnp.where(unused, nblocks - 1, blk)
    ex_ref[...] = jnp.where(unused, last_e, exv)
    lo_ref[...] = lo
    hi_ref[...] = hi


def _slots(cnt, e, r, tb, nwp):
    n = e.shape[1]
    full = lambda shape: pl.BlockSpec(shape, lambda: (0, 0))
    return pl.pallas_call(
        functools.partial(_slots_kernel, tb=tb, nblocks=2 * n // tb),
        in_specs=[pl.BlockSpec(memory_space=pltpu.SMEM), full((2, n)), full((2, n))],
        out_specs=[full((2, n))] + [full((1, nwp))] * 4,
        out_shape=[jax.ShapeDtypeStruct((2, n), jnp.int32)] + [jax.ShapeDtypeStruct((1, nwp), jnp.int32)] * 4,
        name="moe_slots",
    )(cnt, e, r)


def _dispatch_kernel(d_ref, x_ref, xs_ref, sem, *, tm):
    def issue(r, c):
        for k in range(2):
            pltpu.make_async_copy(x_ref.at[pl.ds(r, 1)], xs_ref.at[pl.ds(d_ref[0, 0, k * tm + r], 1)],
                                  sem.at[k]).start(priority=k)
        return c

    lax.fori_loop(0, tm, issue, 0, unroll=ROW_UNROLL)
    for k in range(2):
        pltpu.make_async_copy(x_ref, xs_ref.at[pl.ds(0, tm)], sem.at[k]).wait()


def _dispatch(dest3, x1, nslots, tm):
    n = x1.shape[0]
    return pl.pallas_call(
        functools.partial(_dispatch_kernel, tm=tm),
        grid=(n // tm,),
        in_specs=[
            pl.BlockSpec((1, 1, 2 * tm), lambda i: (i, 0, 0), memory_space=pltpu.SMEM),
            pl.BlockSpec((tm, D_MODEL), lambda i: (i, 0)),
        ],
        out_specs=pl.BlockSpec(memory_space=pl.ANY),
        out_shape=jax.ShapeDtypeStruct((nslots, D_MODEL), F32),
        scratch_shapes=[pltpu.SemaphoreType.DMA((2,))],
        compiler_params=_cparams(("arbitrary",)),
        name="moe_dispatch",
    )(dest3, x1)


def _experts_kernel(blk_ref, ex_ref, lo_ref, hi_ref, xs_ref, wg_ref, wu_ref, wd_ref, y_ref):
    i = pl.program_id(0)
    lo = lo_ref[i]
    hi = hi_ref[i]
    opens_block = (i == 0) | (blk_ref[i] != blk_ref[jnp.maximum(i - 1, 0)])

    @pl.when(lo < hi)
    def _():
        xb = xs_ref[...].astype(BF16)
        g = _dot(xb, wg_ref[...])
        u = _dot(xb, wu_ref[...])
        hid = (g / (1.0 + jnp.exp(-g))) * u
        y = _dot(hid.astype(BF16), wd_ref[...])
        row = lax.broadcasted_iota(jnp.int32, (y.shape[0], 1), 0)
        y = jnp.where((row >= lo) & (row < hi), y, 0.0)

        @pl.when(opens_block)
        def _():
            y_ref[...] = y

        @pl.when(jnp.logical_not(opens_block))
        def _():
            y_ref[...] += y


def _experts(blk, ex, lo, hi, xs, wg, wu, wd, layer, tb, nwork):
    nslots = xs.shape[0]
    rows = lambda i, blk, ex, lo, hi: (blk[i], 0)
    wmap = lambda i, blk, ex, lo, hi: (layer, ex[i], 0, 0)
    return pl.pallas_call(
        _experts_kernel,
        grid_spec=pltpu.PrefetchScalarGridSpec(
            num_scalar_prefetch=4,
            grid=(nwork,),
            in_specs=[
                pl.BlockSpec((tb, D_MODEL), rows),
                pl.BlockSpec((None, None, D_MODEL, EXPERT_FF), wmap),
                pl.BlockSpec((None, None, D_MODEL, EXPERT_FF), wmap),
                pl.BlockSpec((None, None, EXPERT_FF, D_MODEL), wmap),
            ],
            out_specs=pl.BlockSpec((tb, D_MODEL), rows),
        ),
        out_shape=jax.ShapeDtypeStruct((nslots, D_MODEL), F32),
        compiler_params=_cparams(("arbitrary",)),
        name="moe_experts",
    )(blk, ex, lo, hi, xs, wg, wu, wd)


def _combine_kernel(dc_ref, dn_ref, x_ref, gt_ref, lg_ref, lb_ref, y_ref, o_ref, buf, sem, *, tm, alpha):
    i = pl.program_id(0)
    more = i + 1 < pl.num_programs(0)

    def gather(d_ref, s):
        def issue(r, c):
            for k in range(2):
                pltpu.make_async_copy(y_ref.at[pl.ds(d_ref[0, 0, k * tm + r], 1)], buf.at[2 * s + k, pl.ds(r, 1)],
                                      sem.at[s, k]).start(priority=k)
            return c

        lax.fori_loop(0, tm, issue, 0, unroll=ROW_UNROLL)

    @pl.when(i == 0)
    def _():
        gather(dc_ref, 0)

    for s in range(2):
        @pl.when(i % 2 == s)
        def _():
            @pl.when(more)
            def _():
                gather(dn_ref, 1 - s)

            for k in range(2):
                pltpu.make_async_copy(y_ref.at[pl.ds(0, tm)], buf.at[2 * s + k], sem.at[s, k]).wait()
            ffn = buf[2 * s] * gt_ref[:, 0:1] + buf[2 * s + 1] * gt_ref[:, 1:2]
            o_ref[...] = _layer_norm(alpha * x_ref[...] + ffn, lg_ref[...], lb_ref[...])


def _combine(dest3, x1, gates_t, lg, lb, yb, alpha, tm):
    n = x1.shape[0]
    nt = n // tm
    const = lambda i: (0, 0)
    row = lambda i: (i, 0)
    return pl.pallas_call(
        functools.partial(_combine_kernel, tm=tm, alpha=alpha),
        grid=(nt,),
        in_specs=[
            pl.BlockSpec((1, 1, 2 * tm), lambda i: (i, 0, 0), memory_space=pltpu.SMEM),
            pl.BlockSpec((1, 1, 2 * tm), lambda i: (jnp.minimum(i + 1, nt - 1), 0, 0), memory_space=pltpu.SMEM),
            pl.BlockSpec((tm, D_MODEL), row),
            pl.BlockSpec((tm, 2), row),
            pl.BlockSpec(lg.shape, const),
            pl.BlockSpec(lb.shape, const),
            pl.BlockSpec(memory_space=pl.ANY),
        ],
        out_specs=pl.BlockSpec((tm, D_MODEL), row),
        out_shape=jax.ShapeDtypeStruct((n, D_MODEL), F32),
        scratch_shapes=[pltpu.VMEM((4, tm, D_MODEL), F32), pltpu.SemaphoreType.DMA((2, 2))],
        compiler_params=_cparams(("arbitrary",)),
        name="moe_combine",
    )(dest3, dest3, x1, gates_t, lg, lb, yb)


def _moe_ln(x1, e, r, gates, cnt, wg, wu, wd, layer, lg, lb, alpha, tm, tb):
    n = x1.shape[0]
    nwork = (2 * n) // tb + N_EXPERTS - 1
    nwp = -(-nwork // LANE) * LANE
    dest, blk, ex, lo, hi = _slots(cnt.astype(jnp.int32).reshape(N_EXPERTS), e, r, tb, nwp)
    dest3 = dest.reshape(2, n // tm, tm).transpose(1, 0, 2).reshape(n // tm, 1, 2 * tm)
    xs = _dispatch(dest3, x1, 2 * n, tm)
    yb = _experts(blk.reshape(nwp), ex.reshape(nwp), lo.reshape(nwp), hi.reshape(nwp), xs, wg, wu, wd,
                  layer, tb, nwork)
    return _combine(dest3, x1, gates.T, lg, lb, yb, alpha, tm)


def _sample_proj_kernel(x_ref, w_ref, gq_ref, wq_ref, gkv_ref, c_ref, s_ref,
                        q_ref, lat_ref, kpe_ref, qb_ref, kb_ref, vb_ref, acc_ref):
    k = pl.program_id(0)

    @pl.when(k == 0)
    def _():
        acc_ref[...] = jnp.zeros_like(acc_ref)

    acc_ref[...] += _dot(x_ref[...].astype(BF16), w_ref[...].astype(BF16))

    @pl.when(k == pl.num_programs(0) - 1)
    def _():
        cos = c_ref[...]
        sin = s_ref[...]
        cqn = _rms(acc_ref[:, :Q_LORA], gq_ref[...])
        q = _dot(cqn.astype(BF16), wq_ref[...])
        q_ref[:, :H_MLA * NOPE_DIM] = q[:, :H_MLA * NOPE_DIM]
        qr = q[:, H_MLA * NOPE_DIM:]
        q_ref[:, H_MLA * NOPE_DIM:] = qr * cos + _swap32(qr) * sin
        lat_ref[...] = _rms(acc_ref[:, Q_LORA:Q_LORA + KV_LORA], gkv_ref[...])
        a = Q_LORA + KV_LORA
        kv = acc_ref[:, a:a + LANE]
        kpe_ref[...] = (kv * cos[:, :LANE] + _swap32(kv) * sin[:, :LANE])[:, :ROPE_DIM]
        a += ROPE_DIM
        qb_ref[...] = acc_ref[:, a:a + BAND_WIDTH]
        kb_ref[...] = acc_ref[:, a + BAND_WIDTH:a + 2 * BAND_WIDTH]
        vb_ref[...] = acc_ref[:, a + 2 * BAND_WIDTH:a + 3 * BAND_WIDTH]


def _sample_proj(x, w_in, layer, gq, wq, gkv, cos, sin, tk):
    m = x.shape[0]
    width = w_in.shape[2]
    const = lambda k: (0, 0)
    shapes = [(m, wq.shape[1]), (m, KV_LORA), (m, ROPE_DIM), (m, BAND_WIDTH), (m, BAND_WIDTH), (m, BAND_WIDTH)]
    return pl.pallas_call(
        _sample_proj_kernel,
        grid=(D_MODEL // tk,),
        in_specs=[
            pl.BlockSpec((m, tk), lambda k: (0, k)),
            pl.BlockSpec((None, tk, width), lambda k: (layer, k, 0)),
            pl.BlockSpec(gq.shape, const),
            pl.BlockSpec(wq.shape, const),
            pl.BlockSpec(gkv.shape, const),
            pl.BlockSpec(cos.shape, const),
            pl.BlockSpec(sin.shape, const),
        ],
        out_specs=[pl.BlockSpec(s, const) for s in shapes],
        out_shape=[jax.ShapeDtypeStruct(s, F32) for s in shapes],
        scratch_shapes=[pltpu.VMEM((m, width), F32)],
        compiler_params=_cparams(("arbitrary",)),
        name="sample_proj",
    )(x, w_in, gq, wq, gkv, cos, sin)


def _softmax2(sc, sn):
    m = jnp.maximum(jnp.max(sc, axis=-1, keepdims=True), jnp.max(sn, axis=-1, keepdims=True))
    ec = jnp.exp(sc - m)
    en = jnp.exp(sn - m)
    l = jnp.sum(ec, axis=-1, keepdims=True) + jnp.sum(en, axis=-1, keepdims=True)
    return (ec / l).astype(BF16), (en / l).astype(BF16)


def _sample_mla_kernel(q_ref, lat_ref, kpe_ref, cl_ref, cr_ref, wk_ref, wv_ref, o_ref, *, scale):
    lat_c = cl_ref[...].astype(BF16)
    lat_n = lat_ref[...].astype(BF16)
    kpe_c = cr_ref[...].astype(BF16)
    kpe_n = kpe_ref[...].astype(BF16)
    kc = _dot(lat_c, wk_ref[...]).astype(BF16)
    vc = _dot(lat_c, wv_ref[...]).astype(BF16)
    kn = _dot(lat_n, wk_ref[...]).astype(BF16)
    vn = _dot(lat_n, wv_ref[...]).astype(BF16)
    for hd in range(H_MLA):
        a = hd * NOPE_DIM
        b = H_MLA * NOPE_DIM + hd * ROPE_DIM
        qn = q_ref[:, a:a + NOPE_DIM].astype(BF16)
        qr = q_ref[:, b:b + ROPE_DIM].astype(BF16)
        sc = (_dot_nt(qn, kc[:, a:a + NOPE_DIM]) + _dot_nt(qr, kpe_c)) * scale
        sn = (_dot_nt(qn, kn[:, a:a + NOPE_DIM]) + _dot_nt(qr, kpe_n)) * scale
        pc, pn = _softmax2(sc, sn)
        o_ref[:, a:a + V_DIM] = _dot(pc, vc[:, a:a + V_DIM]) + _dot(pn, vn[:, a:a + V_DIM])


def _sample_mla(q, lat, kpe, cache_lat, cache_rope, wk, wv, layer, t):
    m = q.shape[0]
    batch = m // t
    past = cache_lat.shape[2]
    row = lambda b: (b, 0)
    const = lambda b: (0, 0)
    return pl.pallas_call(
        functools.partial(_sample_mla_kernel, scale=(NOPE_DIM + ROPE_DIM) ** -0.5),
        grid=(batch,),
        in_specs=[
            pl.BlockSpec((t, q.shape[1]), row),
            pl.BlockSpec((t, KV_LORA), row),
            pl.BlockSpec((t, ROPE_DIM), row),
            pl.BlockSpec((None, None, past, KV_LORA), lambda b: (layer, b, 0, 0)),
            pl.BlockSpec((None, None, past, ROPE_DIM), lambda b: (layer, b, 0, 0)),
            pl.BlockSpec((None, KV_LORA, MLA_WIDTH), lambda b: (layer, 0, 0)),
            pl.BlockSpec((None, KV_LORA, MLA_WIDTH), lambda b: (layer, 0, 0)),
        ],
        out_specs=pl.BlockSpec((t, MLA_WIDTH), row),
        out_shape=jax.ShapeDtypeStruct((m, MLA_WIDTH), F32),
        compiler_params=_cparams(("parallel",)),
        name="sample_mla",
    )(q, lat, kpe, cache_lat, cache_rope, wk, wv)


def _sample_band_kernel(q_ref, kn_ref, vn_ref, ck_ref, cv_ref, bc_ref, bn_ref, o_ref, *, scale):
    for hd in range(H_BAND):
        a = hd * D_BAND
        q = q_ref[:, a:a + D_BAND].astype(BF16)
        sc = _dot_nt(q, ck_ref[:, a:a + D_BAND].astype(BF16)) * scale + bc_ref[hd]
        sn = _dot_nt(q, kn_ref[:, a:a + D_BAND].astype(BF16)) * scale + bn_ref[hd]
        pc, pn = _softmax2(sc, sn)
        o_ref[:, a:a + D_BAND] = (_dot(pc, cv_ref[:, a:a + D_BAND].astype(BF16))
                                  + _dot(pn, vn_ref[:, a:a + D_BAND].astype(BF16)))


def _sample_band(q, kn, vn, cache_k, cache_v, bias_c, bias_n, layer, t):
    m = q.shape[0]
    batch = m // t
    r = cache_k.shape[2]
    row = lambda b: (b, 0)
    cache = lambda b: (layer, b, 0, 0)
    return pl.pallas_call(
        functools.partial(_sample_band_kernel, scale=D_BAND ** -0.5),
        grid=(batch,),
        in_specs=[
            pl.BlockSpec((t, BAND_WIDTH), row),
            pl.BlockSpec((t, BAND_WIDTH), row),
            pl.BlockSpec((t, BAND_WIDTH), row),
            pl.BlockSpec((None, None, r, BAND_WIDTH), cache),
            pl.BlockSpec((None, None, r, BAND_WIDTH), cache),
            pl.BlockSpec(bias_c.shape, lambda b: (0, 0, 0)),
            pl.BlockSpec(bias_n.shape, lambda b: (0, 0, 0)),
        ],
        out_specs=pl.BlockSpec((t, BAND_WIDTH), row),
        out_shape=jax.ShapeDtypeStruct((m, BAND_WIDTH), F32),
        compiler_params=_cparams(("parallel",)),
        name="sample_band",
    )(q, kn, vn, cache_k, cache_v, bias_c, bias_n)


def _spread(w):
    z = jnp.zeros(w.shape[:-1] + (HALF_ROPE,), w.dtype)
    return jnp.concatenate([w[..., :HALF_ROPE], z, w[..., HALF_ROPE:], z], axis=-1)


def _rope_angles(pos):
    inv = 1.0 / (ROPE_THETA ** (jnp.arange(0, ROPE_DIM, 2, dtype=F32) / ROPE_DIM))
    ang = pos.astype(F32)[:, None] * inv[None, :]
    return jnp.cos(ang), jnp.sin(ang)


def _band_bias_kernel(line_ref, o_ref, *, q0, k0, keys_first):
    nr, nc = o_ref.shape
    table = pltpu.roll(jnp.broadcast_to(line_ref[...], (nr, line_ref.shape[1])), 0, 1, stride=1, stride_axis=0)
    row = lax.broadcasted_iota(jnp.int32, (nr, nc), 0)
    col = lax.broadcasted_iota(jnp.int32, (nr, nc), 1)
    shift = CHUNK * (1 << 20)
    kc = ((row if keys_first else col) + (k0 + shift)) // CHUNK
    qc = ((col if keys_first else row) + (q0 + shift)) // CHUNK
    o_ref[...] = jnp.where((kc <= qc) & (kc >= qc - LEFT_CHUNKS), table[:, :nc], NEG)


def _band_bias_table(rel_bias, q0, nq, k0, nk, keys_first):
    heads = rel_bias.shape[0]
    nr, nc = (nk, nq) if keys_first else (nq, nk)
    period = -(-(nr + nc) // LANE) * LANE
    delta = np.arange(period)
    delta = np.where(delta < nc, delta, delta - period)
    qk = (q0 - k0) + (delta if keys_first else -delta)
    line = rel_bias[:, np.clip(qk, -REL_CLIP, REL_CLIP) + REL_CLIP].reshape(heads, 1, period)
    return pl.pallas_call(
        functools.partial(_band_bias_kernel, q0=q0, k0=k0, keys_first=keys_first),
        grid=(heads,),
        in_specs=[pl.BlockSpec((None, 1, period), lambda h: (h, 0, 0))],
        out_specs=pl.BlockSpec((None, nr, nc), lambda h: (h, 0, 0)),
        out_shape=jax.ShapeDtypeStruct((heads, nr, nc), F32),
        compiler_params=_cparams(("parallel",)),
        name="band_bias",
    )(line)


def kernel(x_prompt, x_sample, cache_mla_latent, cache_mla_rope, cache_band_k, cache_band_v, w_in, g_q_lat, w_q_up, g_kv_lat, w_k_up, w_v_up, rel_bias, g_out_mla, g_out_band, w_out, ln1_g, ln1_b, w_router, b_router, w_gate, w_up, w_down, ln2_g, ln2_b):
    depth = w_in.shape[0]
    batch, seq, _ = x_prompt.shape
    dec_batch, dec_seq, _ = x_sample.shape
    past = cache_mla_latent.shape[2]
    band_r = cache_band_k.shape[2]
    alpha = float((2 * depth) ** 0.25)
    n = batch * seq
    m = dec_batch * dec_seq
    mla_scale = (NOPE_DIM + ROPE_DIM) ** -0.5 * LOG2E
    band_scale = D_BAND ** -0.5 * LOG2E
    a0 = Q_LORA + KV_LORA
    a1 = a0 + ROPE_DIM

    w_a = jnp.concatenate([w_in[:, :, :a0], _spread(w_in[:, :, a0:a1])], axis=-1).astype(BF16)
    w_band = jnp.concatenate([w_in[:, :, a1:a1 + BAND_WIDTH] * band_scale, w_in[:, :, a1 + BAND_WIDTH:]],
                             axis=-1).astype(BF16)
    w_band_vt = w_in[:, :, a1 + 2 * BAND_WIDTH:].transpose(0, 2, 1).astype(BF16)
    wq4 = w_q_up.reshape(depth, Q_LORA, H_MLA, NOPE_DIM + ROPE_DIM)
    wq_pad = jnp.concatenate([wq4[..., :NOPE_DIM], _spread(wq4[..., NOPE_DIM:])], axis=-1) * mla_scale
    wq_pad = wq_pad.reshape(depth, Q_LORA, H_MLA * QK_PAD).astype(BF16)
    wq_s = jnp.concatenate([wq4[..., :NOPE_DIM].reshape(depth, Q_LORA, -1),
                            wq4[..., NOPE_DIM:].reshape(depth, Q_LORA, -1)], axis=-1).astype(BF16)
    wk4 = w_k_up.reshape(depth, KV_LORA, H_MLA, NOPE_DIM)
    wk_pad = jnp.concatenate([wk4, jnp.zeros_like(wk4)], axis=-1).reshape(depth, KV_LORA, H_MLA * QK_PAD)
    wk_pad = wk_pad.astype(BF16)
    wk_b = w_k_up.astype(BF16)
    wv_b = w_v_up.astype(BF16)
    wv_t = w_v_up.transpose(0, 2, 1).astype(BF16)
    w_out_b = w_out.astype(BF16)
    wg_b = w_gate.astype(BF16)
    wu_b = w_up.astype(BF16)
    wd_b = w_down.astype(BF16)
    wr_tb = w_router.T.astype(BF16)
    br = b_router.reshape(N_EXPERTS, 1)
    row2 = lambda a: a.reshape(depth, 1, -1)
    gq, gkv, ga, gb = row2(g_q_lat), row2(g_kv_lat), row2(g_out_mla), row2(g_out_band)
    l1g, l1b, l2g, l2b = row2(ln1_g), row2(ln1_b), row2(ln2_g), row2(ln2_b)

    pos_p = jnp.arange(seq, dtype=jnp.int32)
    cos_p, sin_p = _rope_angles(pos_p)
    z = jnp.zeros_like(cos_p)
    cos_sp = jnp.concatenate([cos_p, z, cos_p, z], axis=-1)
    sin_sp = jnp.concatenate([-sin_p, z, sin_p, z], axis=-1)
    pos_s = past + jnp.arange(dec_seq, dtype=jnp.int32)
    cos_s, sin_s = _rope_angles(pos_s)
    cos_s = jnp.tile(jnp.concatenate([cos_s, cos_s], axis=-1), (dec_batch, H_MLA))
    sin_s = jnp.tile(jnp.concatenate([-sin_s, sin_s], axis=-1), (dec_batch, H_MLA))

    tm_p = 256
    states = [[], [], [], [], [], [], [], []]
    xp = x_prompt.reshape(n, D_MODEL)
    xs = x_sample.reshape(m, D_MODEL)
    ck = cache_band_k.reshape(depth, dec_batch, band_r, BAND_WIDTH)
    cv = cache_band_v.reshape(depth, dec_batch, band_r, BAND_WIDTH)
    for l in range(depth):
        q, k, vt, lat, kpe = _mla_proj(xp, w_a[l], gq[l], wq_pad[l], gkv[l], wk_pad[l], wv_t[l],
                                       cos_sp, sin_sp, seq, tm_p, BAND_PAST)
        qb, kb, vbt, ks, vs = _band_proj(xp, w_band[l], w_band_vt[l], seq, tm_p)
        oa = _mla_attn(q, k, vt, seq, BAND_PAST)
        bias_p = _band_bias_table(rel_bias[l] * LOG2E, 0, BAND_PAST, -BAND_PAST, 2 * BAND_PAST, True)
        ob = _band_attn(qb, kb, vbt, bias_p, seq)
        x1, e, r, gates, cnt = _out_route(oa, ob, ga[l], gb[l], w_out_b[l], xp, l1g[l], l1b[l], wr_tb, br,
                                          alpha, 2 * tm_p, tm_p)
        xp = _moe_ln(x1, e, r, gates, cnt, wg_b, wu_b, wd_b, l, l2g[l], l2b[l], alpha, tm_p, 256)
        for acc, s in zip(states[:4], (lat, kpe, ks, vs)):
            acc.append(s)
        sq, slat, skpe, sqb, skb, svb = _sample_proj(xs, w_in, l, gq[l], wq_s[l], gkv[l], cos_s, sin_s, 256)
        soa = _sample_mla(sq, slat, skpe, cache_mla_latent, cache_mla_rope, wk_b, wv_b, l, dec_seq)
        bias_c = _band_bias_table(rel_bias[l], past, dec_seq, past - band_r, band_r, False)
        bias_n = _band_bias_table(rel_bias[l], past, dec_seq, past, dec_seq, False)
        sob = _sample_band(sqb, skb, svb, ck, cv, bias_c, bias_n, l, dec_seq)
        sx1, se, sr, sgates, scnt = _out_route(soa, sob, ga[l], gb[l], w_out_b[l], xs, l1g[l], l1b[l], wr_tb, br,
                                               alpha, m, m)
        xs = _moe_ln(sx1, se, sr, sgates, scnt, wg_b, wu_b, wd_b, l, l2g[l], l2b[l], alpha, m, 128)
        for acc, s in zip(states[4:], (slat, skpe, skb, svb)):
            acc.append(s)

    st = [jnp.stack(a) for a in states]
    return (xp.reshape(batch, seq, D_MODEL), xs.reshape(dec_batch, dec_seq, D_MODEL),
            st[0].reshape(depth, batch, seq, KV_LORA), st[1].reshape(depth, batch, seq, ROPE_DIM),
            st[2].reshape(depth, batch, BAND_PAST, H_BAND, D_BAND),
            st[3].reshape(depth, batch, BAND_PAST, H_BAND, D_BAND),
            st[4].reshape(depth, dec_batch, dec_seq, KV_LORA), st[5].reshape(depth, dec_batch, dec_seq, ROPE_DIM),
            st[6].reshape(depth, dec_batch, dec_seq, H_BAND, D_BAND),
            st[7].reshape(depth, dec_batch, dec_seq, H_BAND, D_BAND))
```

```python
import functools

import jax
import jax.numpy as jnp
import numpy as np
from jax import lax
from jax.experimental import pallas as pl
from jax.experimental.pallas import tpu as pltpu

D_MODEL = 2048
CHUNK = 64
H_MLA = 8
NOPE_DIM = 128
ROPE_DIM = 64
HALF_ROPE = ROPE_DIM // 2
V_DIM = 128
Q_LORA = 512
KV_LORA = 256
ROPE_THETA = 10000.0
H_BAND = 8
D_BAND = 128
LEFT_CHUNKS = 8
BAND_PAST = LEFT_CHUNKS * CHUNK
REL_CLIP = 128
MLA_WIDTH = H_MLA * V_DIM
BAND_WIDTH = H_BAND * D_BAND
QK_PAD = 256
N_EXPERTS = 16
N_GROUPS = 4
GROUP_SIZE = 4
EXPERT_FF = 512
RMS_EPS = 1e-6
LN_EPS = 1e-5
NEG = -1e30
LOG2E = 1.4426950408889634
LANE = 128
VMEM_LIMIT = 56 * 1024 * 1024

BF16_ROWS = 16
VT_ROWS = V_DIM + BF16_ROWS
ROW_UNROLL = 8

F32 = jnp.float32
BF16 = jnp.bfloat16


def _cparams(sem):
    return pltpu.CompilerParams(dimension_semantics=sem, vmem_limit_bytes=VMEM_LIMIT)


def _dot(a, b):
    return jnp.dot(a, b, preferred_element_type=F32)


def _dot_nt(a, b):
    return lax.dot_general(a, b, (((1,), (1,)), ((), ())), preferred_element_type=F32)


def _rms(x, g):
    return x * lax.rsqrt(jnp.mean(x * x, axis=-1, keepdims=True) + RMS_EPS) * g


def _layer_norm(x, g, b):
    mu = jnp.mean(x, axis=-1, keepdims=True)
    xc = x - mu
    var = jnp.mean(xc * xc, axis=-1, keepdims=True)
    return xc * lax.rsqrt(var + LN_EPS) * g + b


def _swap32(x):
    w = x.shape[-1]
    lane = lax.broadcasted_iota(jnp.int32, x.shape, x.ndim - 1)
    return jnp.where((lane % ROPE_DIM) < HALF_ROPE, pltpu.roll(x, w - HALF_ROPE, x.ndim - 1),
                     pltpu.roll(x, HALF_ROPE, x.ndim - 1))


def _mla_proj_kernel(x_ref, wa_ref, gq_ref, wq_ref, gkv_ref, wk_ref, wv_ref, c_ref, s_ref,
                     q_ref, k_ref, v_ref, lat_ref, kpe_ref):
    xb = x_ref[...].astype(BF16)
    h = _dot(xb, wa_ref[...])
    cqn = _rms(h[:, :Q_LORA], gq_ref[...])
    lat = _rms(h[:, Q_LORA:Q_LORA + KV_LORA], gkv_ref[...])
    kp = h[:, Q_LORA + KV_LORA:]
    cos = c_ref[...]
    sin = s_ref[...]
    kpr = kp * cos + pltpu.roll(kp, 2 * HALF_ROPE, 1) * sin
    lat_ref[...] = lat
    kpe_ref[...] = (kpr + pltpu.roll(kpr, LANE - HALF_ROPE, 1))[:, :ROPE_DIM]
    kprb = kpr.astype(BF16)
    latb = lat.astype(BF16)
    q = _dot(cqn.astype(BF16), wq_ref[...])
    kn = _dot(latb, wk_ref[...])
    vt = _dot_nt(wv_ref[...], latb).astype(BF16)
    ones = jnp.ones((BF16_ROWS, vt.shape[1]), BF16)
    for hd in range(H_MLA):
        v_ref[hd * VT_ROWS:hd * VT_ROWS + V_DIM, :] = vt[hd * V_DIM:(hd + 1) * V_DIM, :]
        v_ref[hd * VT_ROWS + V_DIM:(hd + 1) * VT_ROWS, :] = ones
    for hd in range(H_MLA):
        a = hd * QK_PAD
        qv = q[:, a + NOPE_DIM:a + QK_PAD]
        q_ref[:, a:a + NOPE_DIM] = q[:, a:a + NOPE_DIM].astype(BF16)
        q_ref[:, a + NOPE_DIM:a + QK_PAD] = (qv * cos + pltpu.roll(qv, 2 * HALF_ROPE, 1) * sin).astype(BF16)
        k_ref[:, a:a + NOPE_DIM] = kn[:, a:a + NOPE_DIM].astype(BF16)
        k_ref[:, a + NOPE_DIM:a + QK_PAD] = kprb


def _mla_proj(x, wa, gq, wq, gkv, wk, wv, cos, sin, seq, tm, tk):
    n = x.shape[0]
    tpb = seq // tm
    per_kv = tk // tm
    const = lambda i: (0, 0)
    row = lambda i: (i, 0)
    return pl.pallas_call(
        _mla_proj_kernel,
        grid=(n // tm,),
        in_specs=[
            pl.BlockSpec((tm, D_MODEL), row),
            pl.BlockSpec(wa.shape, const),
            pl.BlockSpec(gq.shape, const),
            pl.BlockSpec(wq.shape, const),
            pl.BlockSpec(gkv.shape, const),
            pl.BlockSpec(wk.shape, const),
            pl.BlockSpec(wv.shape, const),
            pl.BlockSpec((tm, LANE), lambda i: (i % tpb, 0)),
            pl.BlockSpec((tm, LANE), lambda i: (i % tpb, 0)),
        ],
        out_specs=[
            pl.BlockSpec((tm, H_MLA * QK_PAD), row),
            pl.BlockSpec((tm, H_MLA * QK_PAD), row),
            pl.BlockSpec((None, H_MLA * VT_ROWS, tm), lambda i: (i // per_kv, 0, i % per_kv)),
            pl.BlockSpec((tm, KV_LORA), row),
            pl.BlockSpec((tm, ROPE_DIM), row),
        ],
        out_shape=[
            jax.ShapeDtypeStruct((n, H_MLA * QK_PAD), BF16),
            jax.ShapeDtypeStruct((n, H_MLA * QK_PAD), BF16),
            jax.ShapeDtypeStruct((n // tk, H_MLA * VT_ROWS, tk), BF16),
            jax.ShapeDtypeStruct((n, KV_LORA), F32),
            jax.ShapeDtypeStruct((n, ROPE_DIM), F32),
        ],
        compiler_params=_cparams(("parallel",)),
        name="mla_proj",
    )(x, wa, gq, wq, gkv, wk, wv, cos, sin)


def _band_proj_kernel(x_ref, w_ref, wvt_ref, q_ref, k_ref, v_ref, ks_ref, vs_ref, *, tpb, ntail):
    xb = x_ref[...].astype(BF16)
    q_ref[...] = _dot(xb, w_ref[:, :BAND_WIDTH]).astype(BF16)
    k = _dot(xb, w_ref[:, BAND_WIDTH:2 * BAND_WIDTH])
    k_ref[...] = k.astype(BF16)
    vt = _dot_nt(wvt_ref[...], xb).astype(BF16)
    ones = jnp.ones((BF16_ROWS, vt.shape[1]), BF16)
    for hd in range(H_BAND):
        v_ref[hd * VT_ROWS:hd * VT_ROWS + D_BAND, :] = vt[hd * D_BAND:(hd + 1) * D_BAND, :]
        v_ref[hd * VT_ROWS + D_BAND:(hd + 1) * VT_ROWS, :] = ones

    @pl.when(pl.program_id(0) % tpb >= tpb - ntail)
    def _():
        ks_ref[...] = k
        vs_ref[...] = _dot(xb, w_ref[:, 2 * BAND_WIDTH:])


def _band_proj(x, w, wvt, seq, tm):
    n = x.shape[0]
    tpb = seq // tm
    ntail = BAND_PAST // tm
    batch = n // seq
    const = lambda i: (0, 0)
    row = lambda i: (i, 0)
    tail = lambda i: (i // tpb, jnp.maximum(i % tpb - (tpb - ntail), 0), 0)
    return pl.pallas_call(
        functools.partial(_band_proj_kernel, tpb=tpb, ntail=ntail),
        grid=(n // tm,),
        in_specs=[pl.BlockSpec((tm, D_MODEL), row), pl.BlockSpec(w.shape, const), pl.BlockSpec(wvt.shape, const)],
        out_specs=[
            pl.BlockSpec((tm, BAND_WIDTH), row),
            pl.BlockSpec((tm, BAND_WIDTH), row),
            pl.BlockSpec((None, H_BAND * VT_ROWS, tm), lambda i: (i // ntail, 0, i % ntail)),
            pl.BlockSpec((None, tm, BAND_WIDTH), tail),
            pl.BlockSpec((None, tm, BAND_WIDTH), tail),
        ],
        out_shape=[
            jax.ShapeDtypeStruct((n, BAND_WIDTH), BF16),
            jax.ShapeDtypeStruct((n, BAND_WIDTH), BF16),
            jax.ShapeDtypeStruct((n // BAND_PAST, H_BAND * VT_ROWS, BAND_PAST), BF16),
            jax.ShapeDtypeStruct((batch, BAND_PAST, BAND_WIDTH), F32),
            jax.ShapeDtypeStruct((batch, BAND_PAST, BAND_WIDTH), F32),
        ],
        compiler_params=_cparams(("arbitrary",)),
        name="band_proj",
    )(x, w, wvt)


def _attn_out(acc):
    return (acc[:V_DIM, :] / acc[V_DIM:V_DIM + 1, :]).T


def _mla_attn_kernel(qt_ref, jt_ref, q_ref, k_ref, vt_ref, o_ref, m_ref, acc_ref, sa_ref, sb_ref, ma_ref, mb_ref,
                     *, tk, nblk):
    tq = 2 * tk
    kc = lax.broadcasted_iota(jnp.int32, (tk, tk), 0) // CHUNK
    qc = lax.broadcasted_iota(jnp.int32, (tk, tk), 1) // CHUNK
    causal = kc <= qc

    def produce(t, buf, lanes):
        s_ref, bm_ref = buf
        t = jnp.minimum(t, nblk - 1)
        qo = pl.multiple_of(qt_ref[t] * tq + lanes.start, tk)
        ko = pl.multiple_of(jt_ref[t] * tk, tk)
        st = _dot_nt(k_ref[pl.ds(ko, tk), :], q_ref[pl.ds(qo, lanes.stop - lanes.start), :])
        s_ref[:, lanes] = st
        bm_ref[:, lanes] = jnp.max(st, axis=0, keepdims=True)

    def fold(j, st, bmax, lanes):
        m = m_ref[:, lanes]
        m_new = jnp.maximum(m, bmax)
        p = jnp.exp2(st - m_new)
        acc_ref[:, lanes] = jnp.exp2(m - m_new) * acc_ref[:, lanes] + _dot(vt_ref[j], p.astype(BF16))
        m_ref[:, lanes] = m_new

    everything = slice(0, tq)
    lower = slice(0, tk)
    upper = slice(tk, tq)

    def step(t, cur, nxt):
        qi = qt_ref[t]
        j = jt_ref[t]

        @pl.when(j < 2 * qi)
        def _():
            produce(t + 1, nxt, everything)
            fold(j, cur[0][...], cur[1][...], everything)

        @pl.when(j == 2 * qi)
        def _():
            produce(t + 1, nxt, upper)
            st = jnp.where(causal, cur[0][:, lower], NEG)
            fold(j, st, jnp.max(st, axis=0, keepdims=True), lower)
            fold(j, cur[0][:, upper], cur[1][:, upper], upper)

        @pl.when(j == 2 * qi + 1)
        def _():
            produce(t + 1, nxt, everything)
            st = jnp.where(causal, cur[0][:, upper], NEG)
            fold(j, st, jnp.max(st, axis=0, keepdims=True), upper)
            qo = pl.multiple_of(qi * tq, tq)
            o_ref[pl.ds(qo, tq), :] = _attn_out(acc_ref[...]).astype(o_ref.dtype)
            m_ref[...] = jnp.full(m_ref.shape, NEG, F32)
            acc_ref[...] = jnp.zeros(acc_ref.shape, F32)

    m_ref[...] = jnp.full(m_ref.shape, NEG, F32)
    acc_ref[...] = jnp.zeros(acc_ref.shape, F32)
    buf_a = (sa_ref, ma_ref)
    buf_b = (sb_ref, mb_ref)
    produce(0, buf_a, everything)

    def pair(i, c):
        step(2 * i, buf_a, buf_b)
        step(2 * i + 1, buf_b, buf_a)
        return c

    lax.fori_loop(0, nblk // 2, pair, 0)


def _mla_attn(q, k, vt, seq, tk):
    n = q.shape[0]
    batch = n // seq
    tq = 2 * tk
    assert seq % tq == 0
    pairs = [(i, j) for i in range(seq // tq) for j in range(2 * i + 2)]
    qt = jnp.asarray([p[0] for p in pairs], jnp.int32)
    jt = jnp.asarray([p[1] for p in pairs], jnp.int32)
    blk = lambda b, h, qt, jt: (b, h)
    return pl.pallas_call(
        functools.partial(_mla_attn_kernel, tk=tk, nblk=len(pairs)),
        grid_spec=pltpu.PrefetchScalarGridSpec(
            num_scalar_prefetch=2,
            grid=(batch, H_MLA),
            in_specs=[
                pl.BlockSpec((seq, QK_PAD), blk),
                pl.BlockSpec((seq, QK_PAD), blk),
                pl.BlockSpec((seq // tk, VT_ROWS, tk), lambda b, h, qt, jt: (b, h, 0)),
            ],
            out_specs=pl.BlockSpec((seq, V_DIM), blk),
            scratch_shapes=[pltpu.VMEM((1, tq), F32), pltpu.VMEM((VT_ROWS, tq), F32),
                            pltpu.VMEM((tk, tq), F32), pltpu.VMEM((tk, tq), F32),
                            pltpu.VMEM((1, tq), F32), pltpu.VMEM((1, tq), F32)],
        ),
        out_shape=jax.ShapeDtypeStruct((n, MLA_WIDTH), BF16),
        compiler_params=_cparams(("parallel", "parallel")),
        name="mla_attn",
    )(qt, jt, q, k, vt)


def _band_attn_kernel(q_ref, k_ref, vt_ref, b_ref, o_ref, *, nq):
    tq = BAND_PAST
    for i in range(nq):
        q = q_ref[i * tq:(i + 1) * tq, :]
        sc = _dot_nt(k_ref[i * tq:(i + 1) * tq, :], q) + b_ref[tq:, :]
        m = jnp.max(sc, axis=0, keepdims=True)
        if i > 0:
            sp = _dot_nt(k_ref[(i - 1) * tq:i * tq, :], q) + b_ref[:tq, :]
            m = jnp.maximum(m, jnp.max(sp, axis=0, keepdims=True))
            acc = _dot(vt_ref[i - 1], jnp.exp2(sp - m).astype(BF16))
            acc = acc + _dot(vt_ref[i], jnp.exp2(sc - m).astype(BF16))
        else:
            acc = _dot(vt_ref[i], jnp.exp2(sc - m).astype(BF16))
        o_ref[i * tq:(i + 1) * tq, :] = _attn_out(acc).astype(o_ref.dtype)


def _band_attn(q, k, vt, bias_t, seq):
    n = q.shape[0]
    batch = n // seq
    nq = seq // BAND_PAST
    blk = lambda b, h: (b, h)
    return pl.pallas_call(
        functools.partial(_band_attn_kernel, nq=nq),
        grid=(batch, H_BAND),
        in_specs=[
            pl.BlockSpec((seq, D_BAND), blk),
            pl.BlockSpec((seq, D_BAND), blk),
            pl.BlockSpec((nq, VT_ROWS, BAND_PAST), lambda b, h: (b, h, 0)),
            pl.BlockSpec((None, 2 * BAND_PAST, BAND_PAST), lambda b, h: (h, 0, 0)),
        ],
        out_specs=pl.BlockSpec((seq, D_BAND), blk),
        out_shape=jax.ShapeDtypeStruct((n, BAND_WIDTH), BF16),
        compiler_params=_cparams(("parallel", "parallel")),
        name="band_attn",
    )(q, k, vt, bias_t)


def _top2_of4(v, sv):
    best, bi, bs = v[0], jnp.zeros(v[0].shape, jnp.int32), sv[0]
    for i in range(1, GROUP_SIZE):
        c = v[i] > best
        best = jnp.where(c, v[i], best)
        bi = jnp.where(c, i, bi)
        bs = jnp.where(c, sv[i], bs)
    sec = jnp.full(v[0].shape, -jnp.inf, F32)
    si = jnp.zeros(v[0].shape, jnp.int32)
    ss = jnp.zeros(v[0].shape, F32)
    for i in range(GROUP_SIZE):
        c = (bi != i) & (v[i] > sec)
        sec = jnp.where(c, v[i], sec)
        si = jnp.where(c, i, si)
        ss = jnp.where(c, sv[i], ss)
    return best + sec, bi, si, bs, ss


def _route(sel, s):
    groups = []
    for g in range(N_GROUPS):
        rows = range(g * GROUP_SIZE, (g + 1) * GROUP_SIZE)
        groups.append(_top2_of4([sel[i:i + 1, :] for i in rows], [s[i:i + 1, :] for i in rows]))
    score, bi, si, bs, ss = groups[0]
    gi = jnp.zeros(score.shape, jnp.int32)
    for g in range(1, N_GROUPS):
        c = groups[g][0] > score
        score = jnp.where(c, groups[g][0], score)
        gi = jnp.where(c, g, gi)
        bi = jnp.where(c, groups[g][1], bi)
        si = jnp.where(c, groups[g][2], si)
        bs = jnp.where(c, groups[g][3], bs)
        ss = jnp.where(c, groups[g][4], ss)
    den = bs + ss
    return gi * GROUP_SIZE + bi, gi * GROUP_SIZE + si, bs / den, ss / den


def _out_route_kernel(oa_ref, ob_ref, ga_ref, gb_ref, wo_ref, x_ref, lg_ref, lb_ref, wr_ref, br_ref, u_ref,
                      x1_ref, e_ref, r_ref, g_ref, cnt_ref, base_ref, *, alpha):
    @pl.when(pl.program_id(0) == 0)
    def _():
        base_ref[...] = jnp.zeros_like(base_ref)

    ts = u_ref.shape[0]
    base = base_ref[...]
    for sub in range(x_ref.shape[0] // ts):
        rows = slice(sub * ts, (sub + 1) * ts)
        na = _rms(oa_ref[rows, :].astype(F32), ga_ref[...]).astype(BF16)
        nb = _rms(ob_ref[rows, :].astype(F32), gb_ref[...]).astype(BF16)
        mix = _dot(na, wo_ref[:MLA_WIDTH, :]) + _dot(nb, wo_ref[MLA_WIDTH:, :])
        x1 = _layer_norm(alpha * x_ref[rows, :] + mix, lg_ref[...], lb_ref[...])
        x1_ref[rows, :] = x1
        logits = _dot_nt(wr_ref[...], x1.astype(BF16))
        s = 1.0 / (1.0 + jnp.exp(-logits))
        e0, e1, g0, g1 = _route(s + br_ref[...], s)
        e_ref[0:1, rows] = e0
        e_ref[1:2, rows] = e1
        g_ref[0:1, rows] = g0
        g_ref[1:2, rows] = g1
        eid = lax.broadcasted_iota(jnp.int32, logits.shape, 0)
        oh0 = eid == e0
        oh1 = eid == e1
        both = jnp.where(oh0 | oh1, 1.0, 0.0)
        before = _dot(both.astype(BF16), u_ref[...]) + base
        r_ref[0:1, rows] = jnp.sum(jnp.where(oh0, before, 0.0), axis=0, keepdims=True).astype(jnp.int32)
        r_ref[1:2, rows] = jnp.sum(jnp.where(oh1, before, 0.0), axis=0, keepdims=True).astype(jnp.int32)
        base = base + jnp.sum(both, axis=1, keepdims=True)
    base_ref[...] = base
    cnt_ref[...] = base


def _out_route(oa, ob, ga, gb, wo, x, lg, lb, wr, br, alpha, tm, ts):
    n = x.shape[0]
    u = (lax.broadcasted_iota(jnp.int32, (ts, ts), 0) < lax.broadcasted_iota(jnp.int32, (ts, ts), 1)).astype(BF16)
    const = lambda i: (0, 0)
    row = lambda i: (i, 0)
    col = lambda i: (0, i)
    return pl.pallas_call(
        functools.partial(_out_route_kernel, alpha=alpha),
        grid=(n // tm,),
        in_specs=[
            pl.BlockSpec((tm, MLA_WIDTH), row),
            pl.BlockSpec((tm, BAND_WIDTH), row),
            pl.BlockSpec(ga.shape, const),
            pl.BlockSpec(gb.shape, const),
            pl.BlockSpec(wo.shape, const),
            pl.BlockSpec((tm, D_MODEL), row),
            pl.BlockSpec(lg.shape, const),
            pl.BlockSpec(lb.shape, const),
            pl.BlockSpec(wr.shape, const),
            pl.BlockSpec(br.shape, const),
            pl.BlockSpec((ts, ts), const),
        ],
        out_specs=[
            pl.BlockSpec((tm, D_MODEL), row),
            pl.BlockSpec((2, tm), col),
            pl.BlockSpec((2, tm), col),
            pl.BlockSpec((2, tm), col),
            pl.BlockSpec((N_EXPERTS, 1), const),
        ],
        out_shape=[
            jax.ShapeDtypeStruct((n, D_MODEL), F32),
            jax.ShapeDtypeStruct((2, n), jnp.int32),
            jax.ShapeDtypeStruct((2, n), jnp.int32),
            jax.ShapeDtypeStruct((2, n), F32),
            jax.ShapeDtypeStruct((N_EXPERTS, 1), F32),
        ],
        scratch_shapes=[pltpu.VMEM((N_EXPERTS, 1), F32)],
        compiler_params=_cparams(("arbitrary",)),
        name="out_route",
    )(oa, ob, ga, gb, wo, x, lg, lb, wr, br, u)


def _slots_kernel(cnt_ref, e_ref, r_ref, d_ref, blk_ref, ex_ref, lo_ref, hi_ref, *, tb, nblocks):
    shift = tb.bit_length() - 1
    e = e_ref[...]
    dest = r_ref[...]
    item = lax.broadcasted_iota(jnp.int32, blk_ref.shape, 1)
    blk = jnp.zeros(blk_ref.shape, jnp.int32)
    exv = jnp.zeros(blk_ref.shape, jnp.int32)
    lo = jnp.zeros(blk_ref.shape, jnp.int32)
    hi = jnp.zeros(blk_ref.shape, jnp.int32)
    start = jnp.int32(0)
    first = jnp.int32(0)
    last_e = jnp.int32(0)
    for ex in range(N_EXPERTS):
        cnt = cnt_ref[ex]
        end = start + cnt
        sb = lax.shift_right_logical(start, shift)
        nitems = jnp.where(cnt > 0, lax.shift_right_logical(end - 1, shift) - sb + 1, 0)
        dest = dest + jnp.where(e == ex, start, 0)
        mine = (item >= first) & (item < first + nitems)
        b = sb + item - first
        blk = jnp.where(mine, b, blk)
        exv = jnp.where(mine, ex, exv)
        lo = jnp.where(mine, jnp.maximum(start, b * tb) - b * tb, lo)
        hi = jnp.where(mine, jnp.minimum(end, (b + 1) * tb) - b * tb, hi)
        last_e = jnp.where(cnt > 0, ex, last_e)
        start = end
        first = first + nitems
    unused = item >= first
    d_ref[...] = dest
    blk_ref[...] = jnp.where(unused, nblocks - 1, blk)
    ex_ref[...] = jnp.where(unused, last_e, exv)
    lo_ref[...] = lo
    hi_ref[...] = hi


def _slots(cnt, e, r, tb, nwp):
    n = e.shape[1]
    full = lambda shape: pl.BlockSpec(shape, lambda: (0, 0))
    return pl.pallas_call(
        functools.partial(_slots_kernel, tb=tb, nblocks=2 * n // tb),
        in_specs=[pl.BlockSpec(memory_space=pltpu.SMEM), full((2, n)), full((2, n))],
        out_specs=[full((2, n))] + [full((1, nwp))] * 4,
        out_shape=[jax.ShapeDtypeStruct((2, n), jnp.int32)] + [jax.ShapeDtypeStruct((1, nwp), jnp.int32)] * 4,
        name="moe_slots",
    )(cnt, e, r)


def _dispatch_kernel(d_ref, x_ref, xs_ref, sem, *, tm):
    def issue(r, c):
        for k in range(2):
            pltpu.make_async_copy(x_ref.at[pl.ds(r, 1)], xs_ref.at[pl.ds(d_ref[0, 0, k * tm + r], 1)],
                                  sem.at[k]).start(priority=k)
        return c

    lax.fori_loop(0, tm, issue, 0, unroll=ROW_UNROLL)
    for k in range(2):
        pltpu.make_async_copy(x_ref, xs_ref.at[pl.ds(0, tm)], sem.at[k]).wait()


def _dispatch(dest3, x1, nslots, tm):
    n = x1.shape[0]
    return pl.pallas_call(
        functools.partial(_dispatch_kernel, tm=tm),
        grid=(n // tm,),
        in_specs=[
            pl.BlockSpec((1, 1, 2 * tm), lambda i: (i, 0, 0), memory_space=pltpu.SMEM),
            pl.BlockSpec((tm, D_MODEL), lambda i: (i, 0)),
        ],
        out_specs=pl.BlockSpec(memory_space=pl.ANY),
        out_shape=jax.ShapeDtypeStruct((nslots, D_MODEL), F32),
        scratch_shapes=[pltpu.SemaphoreType.DMA((2,))],
        compiler_params=_cparams(("arbitrary",)),
        name="moe_dispatch",
    )(dest3, x1)


def _experts_kernel(blk_ref, ex_ref, lo_ref, hi_ref, xs_ref, wg_ref, wu_ref, wd_ref, y_ref):
    i = pl.program_id(0)
    lo = lo_ref[i]
    hi = hi_ref[i]
    opens_block = (i == 0) | (blk_ref[i] != blk_ref[jnp.maximum(i - 1, 0)])

    @pl.when(lo < hi)
    def _():
        xb = xs_ref[...].astype(BF16)
        g = _dot(xb, wg_ref[...].astype(BF16))
        u = _dot(xb, wu_ref[...].astype(BF16))
        hid = (g / (1.0 + jnp.exp(-g))) * u
        y = _dot(hid.astype(BF16), wd_ref[...].astype(BF16))
        row = lax.broadcasted_iota(jnp.int32, (y.shape[0], 1), 0)
        y = jnp.where((row >= lo) & (row < hi), y, 0.0)

        @pl.when(opens_block)
        def _():
            y_ref[...] = y

        @pl.when(jnp.logical_not(opens_block))
        def _():
            y_ref[...] += y


def _experts(blk, ex, lo, hi, xs, wg, wu, wd, layer, tb, nwork):
    nslots = xs.shape[0]
    rows = lambda i, blk, ex, lo, hi: (blk[i], 0)
    wmap = lambda i, blk, ex, lo, hi: (layer, ex[i], 0, 0)
    return pl.pallas_call(
        _experts_kernel,
        grid_spec=pltpu.PrefetchScalarGridSpec(
            num_scalar_prefetch=4,
            grid=(nwork,),
            in_specs=[
                pl.BlockSpec((tb, D_MODEL), rows),
                pl.BlockSpec((None, None, D_MODEL, EXPERT_FF), wmap),
                pl.BlockSpec((None, None, D_MODEL, EXPERT_FF), wmap),
                pl.BlockSpec((None, None, EXPERT_FF, D_MODEL), wmap),
            ],
            out_specs=pl.BlockSpec((tb, D_MODEL), rows),
        ),
        out_shape=jax.ShapeDtypeStruct((nslots, D_MODEL), F32),
        compiler_params=_cparams(("arbitrary",)),
        name="moe_experts",
    )(blk, ex, lo, hi, xs, wg, wu, wd)


def _combine_kernel(dc_ref, dn_ref, x_ref, gt_ref, lg_ref, lb_ref, y_ref, o_ref, buf, sem, *, tm, alpha):
    i = pl.program_id(0)
    more = i + 1 < pl.num_programs(0)

    def gather(d_ref, s):
        def issue(r, c):
            for k in range(2):
                pltpu.make_async_copy(y_ref.at[pl.ds(d_ref[0, 0, k * tm + r], 1)], buf.at[2 * s + k, pl.ds(r, 1)],
                                      sem.at[s, k]).start(priority=k)
            return c

        lax.fori_loop(0, tm, issue, 0, unroll=ROW_UNROLL)

    @pl.when(i == 0)
    def _():
        gather(dc_ref, 0)

    for s in range(2):
        @pl.when(i % 2 == s)
        def _():
            @pl.when(more)
            def _():
                gather(dn_ref, 1 - s)

            for k in range(2):
                pltpu.make_async_copy(y_ref.at[pl.ds(0, tm)], buf.at[2 * s + k], sem.at[s, k]).wait()
            ffn = buf[2 * s] * gt_ref[:, 0:1] + buf[2 * s + 1] * gt_ref[:, 1:2]
            o_ref[...] = _layer_norm(alpha * x_ref[...] + ffn, lg_ref[...], lb_ref[...])


def _combine(dest3, x1, gates_t, lg, lb, yb, alpha, tm):
    n = x1.shape[0]
    nt = n // tm
    const = lambda i: (0, 0)
    row = lambda i: (i, 0)
    return pl.pallas_call(
        functools.partial(_combine_kernel, tm=tm, alpha=alpha),
        grid=(nt,),
        in_specs=[
            pl.BlockSpec((1, 1, 2 * tm), lambda i: (i, 0, 0), memory_space=pltpu.SMEM),
            pl.BlockSpec((1, 1, 2 * tm), lambda i: (jnp.minimum(i + 1, nt - 1), 0, 0), memory_space=pltpu.SMEM),
            pl.BlockSpec((tm, D_MODEL), row),
            pl.BlockSpec((tm, 2), row),
            pl.BlockSpec(lg.shape, const),
            pl.BlockSpec(lb.shape, const),
            pl.BlockSpec(memory_space=pl.ANY),
        ],
        out_specs=pl.BlockSpec((tm, D_MODEL), row),
        out_shape=jax.ShapeDtypeStruct((n, D_MODEL), F32),
        scratch_shapes=[pltpu.VMEM((4, tm, D_MODEL), F32), pltpu.SemaphoreType.DMA((2, 2))],
        compiler_params=_cparams(("arbitrary",)),
        name="moe_combine",
    )(dest3, dest3, x1, gates_t, lg, lb, yb)


def _moe_ln(x1, e, r, gates, cnt, wg, wu, wd, layer, lg, lb, alpha, tm, tb):
    n = x1.shape[0]
    nwork = (2 * n) // tb + N_EXPERTS - 1
    nwp = -(-nwork // LANE) * LANE
    dest, blk, ex, lo, hi = _slots(cnt.astype(jnp.int32).reshape(N_EXPERTS), e, r, tb, nwp)
    dest3 = dest.reshape(2, n // tm, tm).transpose(1, 0, 2).reshape(n // tm, 1, 2 * tm)
    xs = _dispatch(dest3, x1, 2 * n, tm)
    yb = _experts(blk.reshape(nwp), ex.reshape(nwp), lo.reshape(nwp), hi.reshape(nwp), xs, wg, wu, wd,
                  layer, tb, nwork)
    return _combine(dest3, x1, gates.T, lg, lb, yb, alpha, tm)


def _sample_proj_kernel(x_ref, w_ref, gq_ref, wq_ref, gkv_ref, c_ref, s_ref,
                        q_ref, lat_ref, kpe_ref, qb_ref, kb_ref, vb_ref, acc_ref):
    k = pl.program_id(0)

    @pl.when(k == 0)
    def _():
        acc_ref[...] = jnp.zeros_like(acc_ref)

    acc_ref[...] += _dot(x_ref[...].astype(BF16), w_ref[...].astype(BF16))

    @pl.when(k == pl.num_programs(0) - 1)
    def _():
        cos = c_ref[...]
        sin = s_ref[...]
        cqn = _rms(acc_ref[:, :Q_LORA], gq_ref[...])
        q = _dot(cqn.astype(BF16), wq_ref[...])
        q_ref[:, :H_MLA * NOPE_DIM] = q[:, :H_MLA * NOPE_DIM]
        qr = q[:, H_MLA * NOPE_DIM:]
        q_ref[:, H_MLA * NOPE_DIM:] = qr * cos + _swap32(qr) * sin
        lat_ref[...] = _rms(acc_ref[:, Q_LORA:Q_LORA + KV_LORA], gkv_ref[...])
        a = Q_LORA + KV_LORA
        kv = acc_ref[:, a:a + LANE]
        kpe_ref[...] = (kv * cos[:, :LANE] + _swap32(kv) * sin[:, :LANE])[:, :ROPE_DIM]
        a += ROPE_DIM
        qb_ref[...] = acc_ref[:, a:a + BAND_WIDTH]
        kb_ref[...] = acc_ref[:, a + BAND_WIDTH:a + 2 * BAND_WIDTH]
        vb_ref[...] = acc_ref[:, a + 2 * BAND_WIDTH:a + 3 * BAND_WIDTH]


def _sample_proj(x, w_in, layer, gq, wq, gkv, cos, sin, tk):
    m = x.shape[0]
    width = w_in.shape[2]
    const = lambda k: (0, 0)
    shapes = [(m, wq.shape[1]), (m, KV_LORA), (m, ROPE_DIM), (m, BAND_WIDTH), (m, BAND_WIDTH), (m, BAND_WIDTH)]
    return pl.pallas_call(
        _sample_proj_kernel,
        grid=(D_MODEL // tk,),
        in_specs=[
            pl.BlockSpec((m, tk), lambda k: (0, k)),
            pl.BlockSpec((None, tk, width), lambda k: (layer, k, 0)),
            pl.BlockSpec(gq.shape, const),
            pl.BlockSpec(wq.shape, const),
            pl.BlockSpec(gkv.shape, const),
            pl.BlockSpec(cos.shape, const),
            pl.BlockSpec(sin.shape, const),
        ],
        out_specs=[pl.BlockSpec(s, const) for s in shapes],
        out_shape=[jax.ShapeDtypeStruct(s, F32) for s in shapes],
        scratch_shapes=[pltpu.VMEM((m, width), F32)],
        compiler_params=_cparams(("arbitrary",)),
        name="sample_proj",
    )(x, w_in, gq, wq, gkv, cos, sin)


def _softmax2(sc, sn):
    m = jnp.maximum(jnp.max(sc, axis=-1, keepdims=True), jnp.max(sn, axis=-1, keepdims=True))
    ec = jnp.exp(sc - m)
    en = jnp.exp(sn - m)
    l = jnp.sum(ec, axis=-1, keepdims=True) + jnp.sum(en, axis=-1, keepdims=True)
    return (ec / l).astype(BF16), (en / l).astype(BF16)


def _sample_mla_kernel(q_ref, lat_ref, kpe_ref, cl_ref, cr_ref, wk_ref, wv_ref, o_ref, *, scale):
    lat_c = cl_ref[...].astype(BF16)
    lat_n = lat_ref[...].astype(BF16)
    kpe_c = cr_ref[...].astype(BF16)
    kpe_n = kpe_ref[...].astype(BF16)
    kc = _dot(lat_c, wk_ref[...]).astype(BF16)
    vc = _dot(lat_c, wv_ref[...]).astype(BF16)
    kn = _dot(lat_n, wk_ref[...]).astype(BF16)
    vn = _dot(lat_n, wv_ref[...]).astype(BF16)
    for hd in range(H_MLA):
        a = hd * NOPE_DIM
        b = H_MLA * NOPE_DIM + hd * ROPE_DIM
        qn = q_ref[:, a:a + NOPE_DIM].astype(BF16)
        qr = q_ref[:, b:b + ROPE_DIM].astype(BF16)
        sc = (_dot_nt(qn, kc[:, a:a + NOPE_DIM]) + _dot_nt(qr, kpe_c)) * scale
        sn = (_dot_nt(qn, kn[:, a:a + NOPE_DIM]) + _dot_nt(qr, kpe_n)) * scale
        pc, pn = _softmax2(sc, sn)
        o_ref[:, a:a + V_DIM] = _dot(pc, vc[:, a:a + V_DIM]) + _dot(pn, vn[:, a:a + V_DIM])


def _sample_mla(q, lat, kpe, cache_lat, cache_rope, wk, wv, layer, t):
    m = q.shape[0]
    batch = m // t
    past = cache_lat.shape[2]
    row = lambda b: (b, 0)
    const = lambda b: (0, 0)
    return pl.pallas_call(
        functools.partial(_sample_mla_kernel, scale=(NOPE_DIM + ROPE_DIM) ** -0.5),
        grid=(batch,),
        in_specs=[
            pl.BlockSpec((t, q.shape[1]), row),
            pl.BlockSpec((t, KV_LORA), row),
            pl.BlockSpec((t, ROPE_DIM), row),
            pl.BlockSpec((None, None, past, KV_LORA), lambda b: (layer, b, 0, 0)),
            pl.BlockSpec((None, None, past, ROPE_DIM), lambda b: (layer, b, 0, 0)),
            pl.BlockSpec((None, KV_LORA, MLA_WIDTH), lambda b: (layer, 0, 0)),
            pl.BlockSpec((None, KV_LORA, MLA_WIDTH), lambda b: (layer, 0, 0)),
        ],
        out_specs=pl.BlockSpec((t, MLA_WIDTH), row),
        out_shape=jax.ShapeDtypeStruct((m, MLA_WIDTH), F32),
        compiler_params=_cparams(("parallel",)),
        name="sample_mla",
    )(q, lat, kpe, cache_lat, cache_rope, wk, wv)


def _sample_band_kernel(q_ref, kn_ref, vn_ref, ck_ref, cv_ref, bc_ref, bn_ref, o_ref, *, scale):
    for hd in range(H_BAND):
        a = hd * D_BAND
        q = q_ref[:, a:a + D_BAND].astype(BF16)
        sc = _dot_nt(q, ck_ref[:, a:a + D_BAND].astype(BF16)) * scale + bc_ref[hd]
        sn = _dot_nt(q, kn_ref[:, a:a + D_BAND].astype(BF16)) * scale + bn_ref[hd]
        pc, pn = _softmax2(sc, sn)
        o_ref[:, a:a + D_BAND] = (_dot(pc, cv_ref[:, a:a + D_BAND].astype(BF16))
                                  + _dot(pn, vn_ref[:, a:a + D_BAND].astype(BF16)))


def _sample_band(q, kn, vn, cache_k, cache_v, bias_c, bias_n, layer, t):
    m = q.shape[0]
    batch = m // t
    r = cache_k.shape[2]
    row = lambda b: (b, 0)
    cache = lambda b: (layer, b, 0, 0)
    return pl.pallas_call(
        functools.partial(_sample_band_kernel, scale=D_BAND ** -0.5),
        grid=(batch,),
        in_specs=[
            pl.BlockSpec((t, BAND_WIDTH), row),
            pl.BlockSpec((t, BAND_WIDTH), row),
            pl.BlockSpec((t, BAND_WIDTH), row),
            pl.BlockSpec((None, None, r, BAND_WIDTH), cache),
            pl.BlockSpec((None, None, r, BAND_WIDTH), cache),
            pl.BlockSpec(bias_c.shape, lambda b: (0, 0, 0)),
            pl.BlockSpec(bias_n.shape, lambda b: (0, 0, 0)),
        ],
        out_specs=pl.BlockSpec((t, BAND_WIDTH), row),
        out_shape=jax.ShapeDtypeStruct((m, BAND_WIDTH), F32),
        compiler_params=_cparams(("parallel",)),
        name="sample_band",
    )(q, kn, vn, cache_k, cache_v, bias_c, bias_n)


def _spread(w):
    z = jnp.zeros(w.shape[:-1] + (HALF_ROPE,), w.dtype)
    return jnp.concatenate([w[..., :HALF_ROPE], z, w[..., HALF_ROPE:], z], axis=-1)


def _rope_angles(pos):
    inv = 1.0 / (ROPE_THETA ** (jnp.arange(0, ROPE_DIM, 2, dtype=F32) / ROPE_DIM))
    ang = pos.astype(F32)[:, None] * inv[None, :]
    return jnp.cos(ang), jnp.sin(ang)


def _band_bias_kernel(line_ref, o_ref, *, q0, k0, keys_first):
    nr, nc = o_ref.shape
    table = pltpu.roll(jnp.broadcast_to(line_ref[...], (nr, line_ref.shape[1])), 0, 1, stride=1, stride_axis=0)
    row = lax.broadcasted_iota(jnp.int32, (nr, nc), 0)
    col = lax.broadcasted_iota(jnp.int32, (nr, nc), 1)
    shift = CHUNK * (1 << 20)
    kc = ((row if keys_first else col) + (k0 + shift)) // CHUNK
    qc = ((col if keys_first else row) + (q0 + shift)) // CHUNK
    o_ref[...] = jnp.where((kc <= qc) & (kc >= qc - LEFT_CHUNKS), table[:, :nc], NEG)


def _band_bias_table(rel_bias, q0, nq, k0, nk, keys_first):
    heads = rel_bias.shape[0]
    nr, nc = (nk, nq) if keys_first else (nq, nk)
    period = -(-(nr + nc) // LANE) * LANE
    delta = np.arange(period)
    delta = np.where(delta < nc, delta, delta - period)
    qk = (q0 - k0) + (delta if keys_first else -delta)
    line = rel_bias[:, np.clip(qk, -REL_CLIP, REL_CLIP) + REL_CLIP].reshape(heads, 1, period)
    return pl.pallas_call(
        functools.partial(_band_bias_kernel, q0=q0, k0=k0, keys_first=keys_first),
        grid=(heads,),
        in_specs=[pl.BlockSpec((None, 1, period), lambda h: (h, 0, 0))],
        out_specs=pl.BlockSpec((None, nr, nc), lambda h: (h, 0, 0)),
        out_shape=jax.ShapeDtypeStruct((heads, nr, nc), F32),
        compiler_params=_cparams(("parallel",)),
        name="band_bias",
    )(line)


def kernel(x_prompt, x_sample, cache_mla_latent, cache_mla_rope, cache_band_k, cache_band_v, w_in, g_q_lat, w_q_up, g_kv_lat, w_k_up, w_v_up, rel_bias, g_out_mla, g_out_band, w_out, ln1_g, ln1_b, w_router, b_router, w_gate, w_up, w_down, ln2_g, ln2_b):
    depth = w_in.shape[0]
    batch, seq, _ = x_prompt.shape
    dec_batch, dec_seq, _ = x_sample.shape
    past = cache_mla_latent.shape[2]
    band_r = cache_band_k.shape[2]
    alpha = float((2 * depth) ** 0.25)
    n = batch * seq
    m = dec_batch * dec_seq
    mla_scale = (NOPE_DIM + ROPE_DIM) ** -0.5 * LOG2E
    band_scale = D_BAND ** -0.5 * LOG2E
    a0 = Q_LORA + KV_LORA
    a1 = a0 + ROPE_DIM

    w_a = jnp.concatenate([w_in[:, :, :a0], _spread(w_in[:, :, a0:a1])], axis=-1).astype(BF16)
    w_band = jnp.concatenate([w_in[:, :, a1:a1 + BAND_WIDTH] * band_scale, w_in[:, :, a1 + BAND_WIDTH:]],
                             axis=-1).astype(BF16)
    w_band_vt = w_in[:, :, a1 + 2 * BAND_WIDTH:].transpose(0, 2, 1).astype(BF16)
    wq4 = w_q_up.reshape(depth, Q_LORA, H_MLA, NOPE_DIM + ROPE_DIM)
    wq_pad = jnp.concatenate([wq4[..., :NOPE_DIM], _spread(wq4[..., NOPE_DIM:])], axis=-1) * mla_scale
    wq_pad = wq_pad.reshape(depth, Q_LORA, H_MLA * QK_PAD).astype(BF16)
    wq_s = jnp.concatenate([wq4[..., :NOPE_DIM].reshape(depth, Q_LORA, -1),
                            wq4[..., NOPE_DIM:].reshape(depth, Q_LORA, -1)], axis=-1).astype(BF16)
    wk4 = w_k_up.reshape(depth, KV_LORA, H_MLA, NOPE_DIM)
    wk_pad = jnp.concatenate([wk4, jnp.zeros_like(wk4)], axis=-1).reshape(depth, KV_LORA, H_MLA * QK_PAD)
    wk_pad = wk_pad.astype(BF16)
    wk_b = w_k_up.astype(BF16)
    wv_b = w_v_up.astype(BF16)
    wv_t = w_v_up.transpose(0, 2, 1).astype(BF16)
    w_out_b = w_out.astype(BF16)
    wr_tb = w_router.T.astype(BF16)
    br = b_router.reshape(N_EXPERTS, 1)
    row2 = lambda a: a.reshape(depth, 1, -1)
    gq, gkv, ga, gb = row2(g_q_lat), row2(g_kv_lat), row2(g_out_mla), row2(g_out_band)
    l1g, l1b, l2g, l2b = row2(ln1_g), row2(ln1_b), row2(ln2_g), row2(ln2_b)

    pos_p = jnp.arange(seq, dtype=jnp.int32)
    cos_p, sin_p = _rope_angles(pos_p)
    z = jnp.zeros_like(cos_p)
    cos_sp = jnp.concatenate([cos_p, z, cos_p, z], axis=-1)
    sin_sp = jnp.concatenate([-sin_p, z, sin_p, z], axis=-1)
    pos_s = past + jnp.arange(dec_seq, dtype=jnp.int32)
    cos_s, sin_s = _rope_angles(pos_s)
    cos_s = jnp.tile(jnp.concatenate([cos_s, cos_s], axis=-1), (dec_batch, H_MLA))
    sin_s = jnp.tile(jnp.concatenate([-sin_s, sin_s], axis=-1), (dec_batch, H_MLA))

    tm_p = 256
    states = [[], [], [], [], [], [], [], []]
    xp = x_prompt.reshape(n, D_MODEL)
    xs = x_sample.reshape(m, D_MODEL)
    ck = cache_band_k.reshape(depth, dec_batch, band_r, BAND_WIDTH)
    cv = cache_band_v.reshape(depth, dec_batch, band_r, BAND_WIDTH)
    for l in range(depth):
        q, k, vt, lat, kpe = _mla_proj(xp, w_a[l], gq[l], wq_pad[l], gkv[l], wk_pad[l], wv_t[l],
                                       cos_sp, sin_sp, seq, tm_p, BAND_PAST)
        qb, kb, vbt, ks, vs = _band_proj(xp, w_band[l], w_band_vt[l], seq, tm_p)
        oa = _mla_attn(q, k, vt, seq, BAND_PAST)
        bias_p = _band_bias_table(rel_bias[l] * LOG2E, 0, BAND_PAST, -BAND_PAST, 2 * BAND_PAST, True)
        ob = _band_attn(qb, kb, vbt, bias_p, seq)
        x1, e, r, gates, cnt = _out_route(oa, ob, ga[l], gb[l], w_out_b[l], xp, l1g[l], l1b[l], wr_tb, br,
                                          alpha, 2 * tm_p, tm_p)
        xp = _moe_ln(x1, e, r, gates, cnt, w_gate, w_up, w_down, l, l2g[l], l2b[l], alpha, tm_p, 256)
        for acc, s in zip(states[:4], (lat, kpe, ks, vs)):
            acc.append(s)
        sq, slat, skpe, sqb, skb, svb = _sample_proj(xs, w_in, l, gq[l], wq_s[l], gkv[l], cos_s, sin_s, 256)
        soa = _sample_mla(sq, slat, skpe, cache_mla_latent, cache_mla_rope, wk_b, wv_b, l, dec_seq)
        bias_c = _band_bias_table(rel_bias[l], past, dec_seq, past - band_r, band_r, False)
        bias_n = _band_bias_table(rel_bias[l], past, dec_seq, past, dec_seq, False)
        sob = _sample_band(sqb, skb, svb, ck, cv, bias_c, bias_n, l, dec_seq)
        sx1, se, sr, sgates, scnt = _out_route(soa, sob, ga[l], gb[l], w_out_b[l], xs, l1g[l], l1b[l], wr_tb, br,
                                               alpha, m, m)
        xs = _moe_ln(sx1, se, sr, sgates, scnt, w_gate, w_up, w_down, l, l2g[l], l2b[l], alpha, m, 128)
        for acc, s in zip(states[4:], (slat, skpe, skb, svb)):
            acc.append(s)

    st = [jnp.stack(a) for a in states]
    return (xp.reshape(batch, seq, D_MODEL), xs.reshape(dec_batch, dec_seq, D_MODEL),
            st[0].reshape(depth, batch, seq, KV_LORA), st[1].reshape(depth, batch, seq, ROPE_DIM),
            st[2].reshape(depth, batch, BAND_PAST, H_BAND, D_BAND),
            st[3].reshape(depth, batch, BAND_PAST, H_BAND, D_BAND),
            st[4].reshape(depth, dec_batch, dec_seq, KV_LORA), st[5].reshape(depth, dec_batch, dec_seq, ROPE_DIM),
            st[6].reshape(depth, dec_batch, dec_seq, H_BAND, D_BAND),
            st[7].reshape(depth, dec_batch, dec_seq, H_BAND, D_BAND))
```
